```python
import jax
import jax.numpy as jnp
from jax import lax
import numpy as np

D_MODEL = 1024
BATCH = 8
SEQ = 4096
DEPTH = 4

N_MEM = 256
BLOCK = 128
ROPE_THETA = 10000.0
EPS = 1e-6
NEG = -1e30
MAX_POS_OFFSET = 1024

A_HEADS = 8
A_KV_HEADS = 2
A_GROUP = A_HEADS // A_KV_HEADS
A_HD = 64
A_WINDOW = 128
B_HEADS = 8
B_Q_LORA = 384
B_KV_LORA = 256
B_NOPE = 64
B_ROPE = 32
B_V = 64
C_PATTERNS = ((128, 1), (512, 4), (2048, 16))
N_C_GROUPS = len(C_PATTERNS)
C_HEADS = 8
C_HD = 64
M_HEADS = 4
M_HD = 128

N_BRANCH = 4
BRANCH_W = 512
D_FF = 4 * D_MODEL

IN_SIZES = ((A_HEADS * A_HD, A_KV_HEADS * A_HD, A_KV_HEADS * A_HD, B_Q_LORA, B_KV_LORA, B_ROPE)
            + (C_HEADS * C_HD,) * (3 * N_C_GROUPS)
            + (M_HEADS * M_HD, N_BRANCH * D_MODEL))
N_IN = sum(IN_SIZES)
IN_SPLITS = tuple(int(s) for s in np.cumsum(IN_SIZES)[:-1])

kernel_name = 'hybrid_gated_swa_mla_dilated_mem_block'


def rms_norm(x, g):
    xf = x.astype(jnp.float32)
    y = xf * lax.rsqrt(jnp.mean(xf * xf, axis=-1, keepdims=True) + EPS)
    return (y * g.astype(jnp.float32)).astype(x.dtype)


def rope_tables(positions, dim):
    inv = ROPE_THETA ** (-jnp.arange(0, dim, 2, dtype=jnp.float32) / dim)
    ang = positions.astype(jnp.float32)[..., None] * inv
    return jnp.cos(ang), jnp.sin(ang)


def apply_rope(x, cos, sin):
    xf = x.astype(jnp.float32)
    x1, x2 = jnp.split(xf, 2, axis=-1)
    c = cos[:, :, None, :]
    s = sin[:, :, None, :]
    return jnp.concatenate([x1 * c - x2 * s, x2 * c + x1 * s], axis=-1).astype(x.dtype)


def banded_attention(q, k, v, max_dist, sinks=None):
    n, length, hk, grp, hd = q.shape
    nb = -(-length // BLOCK)
    pad = nb * BLOCK - length
    if pad:
        q = jnp.pad(q, ((0, 0), (0, pad), (0, 0), (0, 0), (0, 0)))
        k = jnp.pad(k, ((0, 0), (0, pad), (0, 0), (0, 0)))
        v = jnp.pad(v, ((0, 0), (0, pad), (0, 0), (0, 0)))
    qb = q.reshape(n, nb, BLOCK, hk, grp, hd)

    def two_blocks(t):
        tb = t.reshape(n, nb, BLOCK, hk, t.shape[-1])
        prev = jnp.pad(tb, ((0, 0), (1, 0), (0, 0), (0, 0), (0, 0)))[:, :-1]
        return jnp.concatenate([prev, tb], axis=2)

    kk = two_blocks(k)
    vv = two_blocks(v)
    s = jnp.einsum('nbqhgd,nbkhd->nbhgqk', qb, kk).astype(jnp.float32) * (hd ** -0.5)
    qi = jnp.arange(BLOCK)[:, None]
    kj = jnp.arange(2 * BLOCK)[None, :]
    dist = qi - kj + BLOCK
    band = (dist >= 0) & (dist <= max_dist)
    valid_prev = (jnp.arange(nb)[:, None, None] > 0) | (kj >= BLOCK)[None]
    mask = band[None] & valid_prev
    s = jnp.where(mask[None, :, None, None], s, NEG)
    lse = jax.nn.logsumexp(s, axis=-1)
    if sinks is not None:
        lse = jnp.logaddexp(lse, sinks.astype(jnp.float32)[None, None, :, :, None])
    p = jnp.exp(s - lse[..., None]).astype(v.dtype)
    o = jnp.einsum('nbhgqk,nbkhd->nbqhgd', p, vv).reshape(n, nb * BLOCK, hk, grp, hd)[:, :length]
    lse = lse.transpose(0, 1, 4, 2, 3).reshape(n, nb * BLOCK, hk, grp)[:, :length]
    return o, lse


def dilated_group(q, k, v, window, dilation):
    b, s, h, hd = q.shape
    length = s // dilation

    def gather(t):
        return t.reshape(b, length, dilation, h, hd).transpose(0, 2, 1, 3, 4).reshape(b * dilation, length, h, hd)

    o, lse = banded_attention(gather(q)[:, :, :, None, :], gather(k), gather(v), window // dilation)
    o = o[:, :, :, 0].reshape(b, dilation, length, h, hd).transpose(0, 2, 1, 3, 4).reshape(b, s, h, hd)
    lse = lse[..., 0].reshape(b, dilation, length, h).transpose(0, 2, 1, 3).reshape(b, s, h)
    return o, lse


def mla_causal_attention(qn, qp, kn, kp, v):
    b, s, h, _ = qn.shape
    nb = s // BLOCK
    scale = (qn.shape[-1] + qp.shape[-1]) ** -0.5
    kpos = jnp.arange(s)

    def one_block(i):
        start = i * BLOCK
        qnb = lax.dynamic_slice_in_dim(qn, start, BLOCK, axis=1)
        qpb = lax.dynamic_slice_in_dim(qp, start, BLOCK, axis=1)
        sc = (jnp.einsum('bqhd,bkhd->bhqk', qnb, kn)
              + jnp.einsum('bqhr,bkr->bhqk', qpb, kp)).astype(jnp.float32) * scale
        qpos = start + jnp.arange(BLOCK)
        sc = jnp.where(kpos[None, :] <= qpos[:, None], sc, NEG)
        p = jax.nn.softmax(sc, axis=-1).astype(v.dtype)
        return jnp.einsum('bhqk,bkhd->bqhd', p, v)

    o = lax.map(one_block, jnp.arange(nb))
    return o.transpose(1, 0, 2, 3, 4).reshape(b, s, h, v.shape[-1])


def memory_attention(q, k, v):
    s = jnp.einsum('bshd,bmhd->bhsm', q, k).astype(jnp.float32) * (q.shape[-1] ** -0.5)
    p = jax.nn.softmax(s, axis=-1).astype(v.dtype)
    return jnp.einsum('bhsm,bmhd->bshd', p, v)


def setup_inputs(seed: int = 0) -> dict:
    key = jax.random.key(seed)
    ks = jax.random.split(key, 32)
    f32 = jnp.float32
    L = DEPTH
    D = D_MODEL

    def nrm(k, shape, scale):
        return jax.random.normal(k, shape, f32) * scale

    def gain(k, shape):
        return 1.0 + 0.1 * jax.random.normal(k, shape, f32)

    x = nrm(ks[0], (BATCH, SEQ, D), 1.0)
    mem = nrm(ks[1], (BATCH, N_MEM, D), 1.0)
    offsets = jax.random.randint(ks[2], (BATCH, 1), 0, MAX_POS_OFFSET, dtype=jnp.int32)
    positions = (offsets + jnp.arange(SEQ, dtype=jnp.int32)[None, :]).astype(jnp.int32)
    return {
        'x': x,
        'mem': mem,
        'positions': positions,
        'g_mix': gain(ks[3], (L, D)),
        'w_in': nrm(ks[4], (L, D, N_IN), D ** -0.5),
        'b_gate': nrm(ks[5], (L, N_BRANCH * D), 0.1),
        'a_qn': gain(ks[6], (L, A_HD)),
        'a_kn': gain(ks[7], (L, A_HD)),
        'a_sink': nrm(ks[8], (L, A_HEADS), 0.5),
        'b_qa_norm': gain(ks[9], (L, B_Q_LORA)),
        'b_kva_norm': gain(ks[10], (L, B_KV_LORA)),
        'b_w_uq': nrm(ks[11], (L, B_Q_LORA, B_HEADS * (B_NOPE + B_ROPE)), B_Q_LORA ** -0.5),
        'b_w_ukv': nrm(ks[12], (L, B_KV_LORA, B_HEADS * (B_NOPE + B_V)), B_KV_LORA ** -0.5),
        'b_qn': gain(ks[13], (L, B_NOPE + B_ROPE)),
        'b_kn': gain(ks[14], (L, B_NOPE + B_ROPE)),
        'c_qn': gain(ks[15], (L, N_C_GROUPS, C_HD)),
        'c_kn': gain(ks[16], (L, N_C_GROUPS, C_HD)),
        'm_g_mem': gain(ks[17], (L, D)),
        'm_w_kv': nrm(ks[18], (L, D, 2 * M_HEADS * M_HD), D ** -0.5),
        'm_qn': gain(ks[19], (L, M_HD)),
        'm_kn': gain(ks[20], (L, M_HD)),
        'w_branch': nrm(ks[21], (L, N_BRANCH, BRANCH_W, D), BRANCH_W ** -0.5),
        'w_out': nrm(ks[22], (L, D, D), D ** -0.5),
        'g_mlp': gain(ks[23], (L, D)),
        'w_up': nrm(ks[24], (L, D, D_FF), D ** -0.5),
        'w_down': nrm(ks[25], (L, D_FF, D), D_FF ** -0.5),
    }


def reference(x, mem, positions, g_mix, w_in, b_gate, a_qn, a_kn, a_sink, b_qa_norm, b_kva_norm,
              b_w_uq, b_w_ukv, b_qn, b_kn, c_qn, c_kn, m_g_mem, m_w_kv, m_qn, m_kn,
              w_branch, w_out, g_mlp, w_up, w_down):
    B, S, D = x.shape
    cos_h, sin_h = rope_tables(positions, A_HD)
    cos_r, sin_r = rope_tables(positions, B_ROPE)
    for l in range(DEPTH):
        h = rms_norm(x, g_mix[l])
        parts = jnp.split(h @ w_in[l], IN_SPLITS, axis=-1)
        a_q, a_k, a_v, b_cq, b_ckv, b_kr = parts[:6]
        c_parts = parts[6:6 + 3 * N_C_GROUPS]
        m_q, gate_pre = parts[6 + 3 * N_C_GROUPS:]

        qa = apply_rope(rms_norm(a_q.reshape(B, S, A_HEADS, A_HD), a_qn[l]), cos_h, sin_h)
        ka = apply_rope(rms_norm(a_k.reshape(B, S, A_KV_HEADS, A_HD), a_kn[l]), cos_h, sin_h)
        va = a_v.reshape(B, S, A_KV_HEADS, A_HD)
        o_a, _ = banded_attention(qa.reshape(B, S, A_KV_HEADS, A_GROUP, A_HD), ka, va,
                                  A_WINDOW - 1, a_sink[l].reshape(A_KV_HEADS, A_GROUP))
        o_a = o_a.reshape(B, S, BRANCH_W)

        q_up = (rms_norm(b_cq, b_qa_norm[l]) @ b_w_uq[l]).reshape(B, S, B_HEADS, B_NOPE + B_ROPE)
        kv_up = (rms_norm(b_ckv, b_kva_norm[l]) @ b_w_ukv[l]).reshape(B, S, B_HEADS, B_NOPE + B_V)
        qn = rms_norm(q_up[..., :B_NOPE], b_qn[l, :B_NOPE])
        qp = apply_rope(rms_norm(q_up[..., B_NOPE:], b_qn[l, B_NOPE:]), cos_r, sin_r)
        kn = rms_norm(kv_up[..., :B_NOPE], b_kn[l, :B_NOPE])
        vb = kv_up[..., B_NOPE:]
        kp = apply_rope(rms_norm(b_kr, b_kn[l, B_NOPE:])[:, :, None, :], cos_r, sin_r)[:, :, 0, :]
        o_b = mla_causal_attention(qn, qp, kn, kp, vb).reshape(B, S, BRANCH_W)

        outs = []
        lses = []
        for g, (win, dil) in enumerate(C_PATTERNS):
            cq, ck, cv = c_parts[3 * g:3 * g + 3]
            qc = apply_rope(rms_norm(cq.reshape(B, S, C_HEADS, C_HD), c_qn[l, g]), cos_h, sin_h)
            kc = apply_rope(rms_norm(ck.reshape(B, S, C_HEADS, C_HD), c_kn[l, g]), cos_h, sin_h)
            vc = cv.reshape(B, S, C_HEADS, C_HD)
            o_g, lse_g = dilated_group(qc, kc, vc, win, dil)
            outs.append(o_g)
            lses.append(lse_g)
        wts = jax.nn.softmax(jnp.stack(lses, axis=0), axis=0).astype(x.dtype)
        o_c = jnp.einsum('gbsh,gbshd->bshd', wts, jnp.stack(outs, axis=0)).reshape(B, S, BRANCH_W)

        mkv = rms_norm(mem, m_g_mem[l]) @ m_w_kv[l]
        mk = rms_norm(mkv[..., :M_HEADS * M_HD].reshape(B, N_MEM, M_HEADS, M_HD), m_kn[l])
        mv = mkv[..., M_HEADS * M_HD:].reshape(B, N_MEM, M_HEADS, M_HD)
        mq = rms_norm(m_q.reshape(B, S, M_HEADS, M_HD), m_qn[l])
        o_m = memory_attention(mq, mk, mv).reshape(B, S, BRANCH_W)

        o = jnp.stack([o_a, o_b, o_c, o_m], axis=2)
        y = jnp.einsum('bsnc,ncd->bsnd', o, w_branch[l])
        gates = jax.nn.sigmoid((gate_pre + b_gate[l]).astype(jnp.float32)).astype(x.dtype)
        gates = gates.reshape(B, S, N_BRANCH, D)
        x = x + jnp.einsum('bsnd,de->bse', gates * y, w_out[l])

        u = rms_norm(x, g_mlp[l]) @ w_up[l]
        x = x + jnp.square(jax.nn.relu(u)) @ w_down[l]
    return x
```

```python
import functools

import numpy as np
import jax
import jax.numpy as jnp
from jax import lax
from jax.experimental import pallas as pl
from jax.experimental.pallas import tpu as pltpu

F32 = jnp.float32
BF16 = jnp.bfloat16

D_MODEL = 1024
N_MEM = 256
BLOCK = 128
ROPE_THETA = 10000.0
EPS = 1e-6
NEG = -1e30

A_HEADS, A_KV_HEADS, A_HD = 8, 2, 64
B_HEADS, B_Q_LORA, B_KV_LORA, B_NOPE, B_ROPE, B_V = 8, 384, 256, 64, 32, 64
C_PATTERNS = ((128, 1), (512, 4), (2048, 16))
C_HEADS, C_HD = 8, 64
M_HEADS, M_HD = 4, 128
N_BRANCH = 4
BRANCH_W = 512
D_FF = 4 * D_MODEL

LANES = 128
MXU_DIM = 256
VMEM_LIMIT = 56 * 1024 * 1024
ROW_TILE = 512
MLA_TILE = 256


def _dot(a, b):
    return jnp.dot(a, b, preferred_element_type=F32)


def _dot_t(a, b):
    return lax.dot_general(a, b, (((1,), (1,)), ((), ())), preferred_element_type=F32)


def _resident(shape):
    nd = len(shape)
    return pl.BlockSpec(shape, lambda *_: (0,) * nd, pipeline_mode=pl.Buffered(1))


def _rows(width, tm=None):
    return pl.BlockSpec((tm or ROW_TILE, width), lambda i: (i, 0))


def _params(n_axes):
    return pltpu.CompilerParams(dimension_semantics=("arbitrary",) * n_axes,
                                vmem_limit_bytes=VMEM_LIMIT)


def _rms_bf16(xf, g):
    y = xf * lax.rsqrt(jnp.mean(xf * xf, axis=-1, keepdims=True) + EPS)
    return (y * g).astype(BF16)


def _seg_norm(z, seg, gain):
    sq = (z * z).astype(BF16)
    w = z.shape[1]
    if w <= MXU_DIM:
        ms = _dot(sq, seg[:w, :w])
    else:
        ms = jnp.concatenate([_dot(sq[:, c:c + MXU_DIM], seg) for c in range(0, w, MXU_DIM)], axis=1)
    return z * lax.rsqrt(ms + EPS) * gain


def _rope(z, c, s1, s2, half):
    w = z.shape[1]
    n = w // LANES

    def tile(t):
        return t if n == 1 else jnp.concatenate([t] * n, axis=1)

    return z * tile(c) + pltpu.roll(z, w - half, 1) * tile(s1) + pltpu.roll(z, half, 1) * tile(s2)


def _proj_a_kernel(x_ref, g_ref, w_ref, seg_ref, gq_ref, gk_ref, c_ref, s1_ref, s2_ref,
                   q_out, k_out, v_out):
    h = _rms_bf16(x_ref[...], g_ref[...])
    z = _dot(h, w_ref[...])
    c, s1, s2 = c_ref[...], s1_ref[...], s2_ref[...]
    seg = seg_ref[...]
    q = _rope(_seg_norm(z[:, :512], seg, gq_ref[...]), c, s1, s2, A_HD // 2)
    k = _rope(_seg_norm(z[:, 512:640], seg, gk_ref[...]), c, s1, s2, A_HD // 2)
    q_out[...] = q.astype(BF16)
    k_out[...] = k.astype(BF16)
    v_out[...] = z[:, 640:768].astype(BF16)


def _proj_a(x, g, w, seg, gq, gk, tabs):
    t = x.shape[0]
    return pl.pallas_call(
        _proj_a_kernel,
        grid=(t // ROW_TILE,),
        in_specs=[_rows(D_MODEL), _resident(g.shape), _resident(w.shape), _resident(seg.shape),
                  _resident(gq.shape), _resident(gk.shape), _rows(LANES), _rows(LANES), _rows(LANES)],
        out_specs=[_rows(512), _rows(128), _rows(128)],
        out_shape=[jax.ShapeDtypeStruct((t, 512), BF16), jax.ShapeDtypeStruct((t, 128), BF16),
                   jax.ShapeDtypeStruct((t, 128), BF16)],
        compiler_params=_params(1), name="proj_a",
    )(x, g, w, seg, gq, gk, *tabs)


def _proj_b_kernel(x_ref, g_ref, w_ref, gqa_ref, gkva_ref, wuq_ref, wuk_ref, wuv_ref, seg_ref,
                   gq_ref, gk_ref, c_ref, s1_ref, s2_ref, q_out, k_out, v_out):
    h = _rms_bf16(x_ref[...], g_ref[...])
    z = _dot(h, w_ref[...])
    cq = _rms_bf16(z[:, :B_Q_LORA], gqa_ref[...])
    ckv = _rms_bf16(z[:, B_Q_LORA:B_Q_LORA + B_KV_LORA], gkva_ref[...])
    kr = z[:, B_Q_LORA + B_KV_LORA:]
    q_up = _dot(cq, wuq_ref[...])
    k_up = _dot(ckv, wuk_ref[...]) + jnp.concatenate([kr] * B_HEADS, axis=1)
    v_out[...] = _dot(ckv, wuv_ref[...]).astype(BF16)
    c, s1, s2 = c_ref[...], s1_ref[...], s2_ref[...]
    seg = seg_ref[...]
    q = _rope(_seg_norm(q_up, seg, gq_ref[...]), c, s1, s2, B_ROPE // 2)
    k = _rope(_seg_norm(k_up, seg, gk_ref[...]), c, s1, s2, B_ROPE // 2)
    q_out[...] = q.astype(BF16)
    k_out[...] = k.astype(BF16)


def _proj_b(x, g, w, gqa, gkva, wuq, wuk, wuv, seg, gq, gk, tabs):
    t = x.shape[0]
    consts = (g, w, gqa, gkva, wuq, wuk, wuv, seg, gq, gk)
    return pl.pallas_call(
        _proj_b_kernel,
        grid=(t // ROW_TILE,),
        in_specs=[_rows(D_MODEL)] + [_resident(a.shape) for a in consts] + [_rows(LANES)] * 3,
        out_specs=[_rows(1024), _rows(1024), _rows(512)],
        out_shape=[jax.ShapeDtypeStruct((t, 1024), BF16), jax.ShapeDtypeStruct((t, 1024), BF16),
                   jax.ShapeDtypeStruct((t, 512), BF16)],
        compiler_params=_params(1), name="proj_b",
    )(x, *consts, *tabs)


def _proj_c_kernel(x_ref, g_ref, w_ref, seg_ref, gq_ref, gk_ref, c_ref, s1_ref, s2_ref, *outs):
    h = _rms_bf16(x_ref[...], g_ref[...])
    c, s1, s2 = c_ref[...], s1_ref[...], s2_ref[...]
    seg = seg_ref[...]
    for grp in range(len(C_PATTERNS)):
        z = _dot(h, w_ref[:, grp * 1536:(grp + 1) * 1536])
        q = _rope(_seg_norm(z[:, :512], seg, gq_ref[grp:grp + 1, :]), c, s1, s2, C_HD // 2)
        k = _rope(_seg_norm(z[:, 512:1024], seg, gk_ref[grp:grp + 1, :]), c, s1, s2, C_HD // 2)
        outs[3 * grp][...] = q.astype(BF16)
        outs[3 * grp + 1][...] = k.astype(BF16)
        outs[3 * grp + 2][...] = z[:, 1024:].astype(BF16)


def _proj_c(x, g, w, seg, gq, gk, tabs):
    t = x.shape[0]
    n_out = 3 * len(C_PATTERNS)
    return pl.pallas_call(
        _proj_c_kernel,
        grid=(t // ROW_TILE,),
        in_specs=[_rows(D_MODEL), _resident(g.shape), _resident(w.shape), _resident(seg.shape),
                  _resident(gq.shape), _resident(gk.shape), _rows(LANES), _rows(LANES), _rows(LANES)],
        out_specs=[_rows(512)] * n_out,
        out_shape=[jax.ShapeDtypeStruct((t, 512), BF16)] * n_out,
        compiler_params=_params(1), name="proj_c",
    )(x, g, w, seg, gq, gk, *tabs)


def _head_norm_128(z, gain):
    parts = []
    for h0 in range(0, z.shape[1], LANES):
        zh = z[:, h0:h0 + LANES]
        parts.append(zh * lax.rsqrt(jnp.mean(zh * zh, axis=-1, keepdims=True) + EPS))
    return jnp.concatenate(parts, axis=1) * gain


def _mem_kv_kernel(mem_ref, g_ref, w_ref, gk_ref, k_out, v_out):
    h = _rms_bf16(mem_ref[...], g_ref[...])
    z = _dot(h, w_ref[...])
    k_out[...] = _head_norm_128(z[:, :512], gk_ref[...]).astype(BF16)
    v_out[...] = z[:, 512:].astype(BF16)


def _mem_kv(mem2d, g, w, gk):
    rows = mem2d.shape[0]
    return pl.pallas_call(
        _mem_kv_kernel,
        grid=(rows // N_MEM,),
        in_specs=[_rows(D_MODEL, N_MEM), _resident(g.shape), _resident(w.shape), _resident(gk.shape)],
        out_specs=[_rows(512, N_MEM), _rows(512, N_MEM)],
        out_shape=[jax.ShapeDtypeStruct((rows, 512), BF16)] * 2,
        compiler_params=_params(1), name="mem_kv",
    )(mem2d, g, w, gk)


def _proj_m_kernel(x_ref, g_ref, w_ref, gq_ref, mk_ref, mv_ref, o_out):
    h = _rms_bf16(x_ref[...], g_ref[...])
    q = _head_norm_128(_dot(h, w_ref[...]), gq_ref[...]).astype(BF16)
    for hd in range(M_HEADS):
        cols = slice(hd * M_HD, (hd + 1) * M_HD)
        s = _dot_t(q[:, cols], mk_ref[:, cols])
        m = jnp.max(s, axis=-1, keepdims=True)
        e = jnp.exp(s - m)
        l = jnp.sum(e, axis=-1, keepdims=True)
        o_out[:, cols] = (_dot(e.astype(BF16), mv_ref[:, cols]) * (1.0 / l)).astype(BF16)


def _proj_m(x, g, w, gq, mk, mv, seq):
    t = x.shape[0]
    per_batch = seq // ROW_TILE
    mem_spec = pl.BlockSpec((N_MEM, 512), lambda i: (i // per_batch, 0))
    return pl.pallas_call(
        _proj_m_kernel,
        grid=(t // ROW_TILE,),
        in_specs=[_rows(D_MODEL), _resident(g.shape), _resident(w.shape), _resident(gq.shape),
                  mem_spec, mem_spec],
        out_specs=_rows(512),
        out_shape=jax.ShapeDtypeStruct((t, 512), BF16),
        compiler_params=_params(1), name="proj_m",
    )(x, g, w, gq, mk, mv)


def _band_kernel(*refs, n_sub, kw, max_dist, use_sink, want_lse):
    refs = list(refs)
    sink_ref = refs.pop(0) if use_sink else None
    q_ref, kc_ref, kp_ref, vc_ref, vp_ref, o_out = refs[:6]
    lse_out = refs[6] if want_lse else None
    qi = lax.broadcasted_iota(jnp.int32, (BLOCK, 2 * BLOCK), 0)
    kj = lax.broadcasted_iota(jnp.int32, (BLOCK, 2 * BLOCK), 1)
    dist = qi - kj + BLOCK
    band = (dist >= 0) & (dist <= max_dist)
    prev_ok = jnp.where(pl.program_id(2) > 0, BLOCK, 0)
    first_mask = band & (kj + prev_ok >= BLOCK)
    lo = lax.broadcasted_iota(jnp.int32, (BLOCK, LANES), 1) < 64
    for j in range(n_sub):
        rows = slice(j * BLOCK, (j + 1) * BLOCK)
        prev = slice((j - 1) * BLOCK, j * BLOCK)
        mask = first_mask if j == 0 else band
        for p in range(4):
            kcol = slice(0, LANES) if kw == LANES else slice(p * LANES, (p + 1) * LANES)
            k_prev = kp_ref[:, kcol] if j == 0 else kc_ref[prev, kcol]
            v_prev = vp_ref[:, kcol] if j == 0 else vc_ref[prev, kcol]
            kk = jnp.concatenate([k_prev, kc_ref[rows, kcol]], axis=0)
            vv = jnp.concatenate([v_prev, vc_ref[rows, kcol]], axis=0)
            qp = q_ref[rows, p * LANES:(p + 1) * LANES]
            outs, lses = [], []
            for hh in range(2):
                qm = jnp.where(lo if hh == 0 else jnp.logical_not(lo), qp, jnp.zeros_like(qp))
                s = jnp.where(mask, _dot_t(qm, kk), NEG)
                m = jnp.max(s, axis=-1, keepdims=True)
                if use_sink:
                    sk = sink_ref[2 * p + hh]
                    m = jnp.maximum(m, sk)
                e = jnp.exp(s - m)
                l = jnp.sum(e, axis=-1, keepdims=True)
                if use_sink:
                    l = l + jnp.exp(sk - m)
                outs.append(_dot(e.astype(BF16), vv) * (1.0 / l))
                lses.append(m + jnp.log(l))
            o_out[rows, p * LANES:(p + 1) * LANES] = jnp.where(lo, outs[0], outs[1]).astype(BF16)
            if want_lse:
                lse_out[rows, p * LANES:(p + 1) * LANES] = jnp.where(
                    lo, jnp.broadcast_to(lses[0], (BLOCK, LANES)), jnp.broadcast_to(lses[1], (BLOCK, LANES)))


def _band_attention(q, k, v, sinks, *, batch, dil, max_dist, want_lse):
    t, kw = k.shape
    seq = t // batch
    length = seq // dil
    tq = min(ROW_TILE, length)
    n_sub = tq // BLOCK
    qv = q.reshape(batch, length, dil * 512)
    kv = k.reshape(batch, length, dil * kw)
    vv = v.reshape(batch, length, dil * kw)
    cur = lambda w: pl.BlockSpec((None, tq, w), lambda b, r, i: (b, i, r))
    prv = pl.BlockSpec((None, BLOCK, kw), lambda b, r, i: (b, jnp.maximum(i * n_sub - 1, 0), r))
    use_sink = sinks is not None
    in_specs = [cur(512), cur(kw), prv, cur(kw), prv]
    args = [qv, kv, kv, vv, vv]
    if use_sink:
        in_specs = [pl.BlockSpec(memory_space=pltpu.SMEM)] + in_specs
        args = [sinks] + args
    out_specs = [cur(512)]
    out_shape = [jax.ShapeDtypeStruct((batch, length, dil * 512), BF16)]
    if want_lse:
        out_specs.append(cur(512))
        out_shape.append(jax.ShapeDtypeStruct((batch, length, dil * 512), F32))
    res = pl.pallas_call(
        functools.partial(_band_kernel, n_sub=n_sub, kw=kw, max_dist=max_dist, use_sink=use_sink,
                          want_lse=want_lse),
        grid=(batch, dil, length // tq),
        in_specs=in_specs, out_specs=out_specs, out_shape=out_shape,
        compiler_params=_params(3), name="band_attention_d%d" % dil,
    )(*args)
    return [r.reshape(t, 512) for r in res]


def _mla_kernel(q_ref, k_ref, v_ref, o_out, *, tile):
    i = pl.program_id(1)
    row = lax.broadcasted_iota(jnp.int32, (tile, tile), 0)
    col = lax.broadcasted_iota(jnp.int32, (tile, tile), 1)
    causal = col <= row
    lo = lax.broadcasted_iota(jnp.int32, (tile, LANES), 1) < 64
    for p in range(B_HEADS // 2):
        vcols = slice(p * LANES, (p + 1) * LANES)
        res = []
        for hh in range(2):
            hcols = slice((2 * p + hh) * LANES, (2 * p + hh + 1) * LANES)
            qh = q_ref[:, hcols]

            def scores(j, hcols=hcols, qh=qh, vcols=vcols):
                start = pl.multiple_of(j * tile, tile)
                return (_dot_t(qh, k_ref[pl.ds(start, tile), hcols]), v_ref[pl.ds(start, tile), vcols])

            s, vt = scores(i)
            s = jnp.where(causal, s, NEG)
            m = jnp.max(s, axis=-1, keepdims=True)
            e = jnp.exp(s - m)
            l = jnp.sum(e, axis=-1, keepdims=True)
            acc = _dot(e.astype(BF16), vt)

            def body(j, carry, scores=scores):
                m, l, acc = carry
                s, vt = scores(j)
                m_new = jnp.maximum(m, jnp.max(s, axis=-1, keepdims=True))
                alpha = jnp.exp(m - m_new)
                e = jnp.exp(s - m_new)
                return (m_new, alpha * l + jnp.sum(e, axis=-1, keepdims=True),
                        alpha * acc + _dot(e.astype(BF16), vt))

            m, l, acc = lax.fori_loop(0, i, body, (m, l, acc))
            res.append(acc * (1.0 / l))
        o_out[:, vcols] = jnp.where(lo, res[0], res[1]).astype(BF16)


def _mla_attention(q, k, v, *, batch):
    t = q.shape[0]
    seq = t // batch
    tile = min(MLA_TILE, seq)
    q3, k3, v3 = (a.reshape(batch, seq, a.shape[1]) for a in (q, k, v))
    o = pl.pallas_call(
        functools.partial(_mla_kernel, tile=tile),
        grid=(batch, seq // tile),
        in_specs=[pl.BlockSpec((None, tile, 1024), lambda b, i: (b, i, 0)),
                  pl.BlockSpec((None, seq, 1024), lambda b, i: (b, 0, 0)),
                  pl.BlockSpec((None, seq, 512), lambda b, i: (b, 0, 0))],
        out_specs=pl.BlockSpec((None, tile, 512), lambda b, i: (b, i, 0)),
        out_shape=jax.ShapeDtypeStruct((batch, seq, 512), BF16),
        compiler_params=_params(2), name="mla_attention",
    )(q3, k3, v3)
    return o.reshape(t, 512)


def _merge_kernel(x_ref, g_ref, wg_ref, bg_ref, oa_ref, ob_ref, oc0_ref, oc1_ref, oc2_ref,
                  l0_ref, l1_ref, l2_ref, om_ref, wb_ref, wo_ref, x_out):
    xf = x_ref[...]
    h = _rms_bf16(xf, g_ref[...])
    l0, l1, l2 = l0_ref[...], l1_ref[...], l2_ref[...]
    mx = jnp.maximum(jnp.maximum(l0, l1), l2)
    e0, e1, e2 = jnp.exp(l0 - mx), jnp.exp(l1 - mx), jnp.exp(l2 - mx)
    o_c = (e0 * oc0_ref[...].astype(F32) + e1 * oc1_ref[...].astype(F32)
           + e2 * oc2_ref[...].astype(F32)) * (1.0 / (e0 + e1 + e2))
    branches = (oa_ref[...], ob_ref[...], o_c.astype(BF16), om_ref[...])
    acc = jnp.zeros(xf.shape, F32)
    for n in range(N_BRANCH):
        cols = slice(n * D_MODEL, (n + 1) * D_MODEL)
        gate = jax.nn.sigmoid(_dot(h, wg_ref[:, cols]) + bg_ref[:, cols])
        acc = acc + gate * _dot(branches[n], wb_ref[n])
    x_out[...] = xf + _dot(acc.astype(BF16), wo_ref[...])


def _merge(x, g, wg, bg, oa, ob, oc, lse, om, wb, wo):
    t = x.shape[0]
    return pl.pallas_call(
        _merge_kernel,
        grid=(t // ROW_TILE,),
        in_specs=[_rows(D_MODEL), _resident(g.shape), _resident(wg.shape), _resident(bg.shape)]
                 + [_rows(512)] * 9 + [_resident(wb.shape), _resident(wo.shape)],
        out_specs=_rows(D_MODEL),
        out_shape=jax.ShapeDtypeStruct((t, D_MODEL), F32),
        compiler_params=_params(1), name="merge",
    )(x, g, wg, bg, oa, ob, *oc, *lse, om, wb, wo)


def _mlp_kernel(x_ref, g_ref, wu_ref, wd_ref, x_out):
    xf = x_ref[...]
    h = _rms_bf16(xf, g_ref[...])
    acc = xf
    for c0 in range(0, D_FF, D_MODEL):
        u = jnp.maximum(_dot(h, wu_ref[:, c0:c0 + D_MODEL]), 0.0)
        acc = acc + _dot((u * u).astype(BF16), wd_ref[c0:c0 + D_MODEL, :])
    x_out[...] = acc


def _mlp(x, g, wu, wd):
    t = x.shape[0]
    return pl.pallas_call(
        _mlp_kernel,
        grid=(t // ROW_TILE,),
        in_specs=[_rows(D_MODEL), _resident(g.shape), _resident(wu.shape), _resident(wd.shape)],
        out_specs=_rows(D_MODEL),
        out_shape=jax.ShapeDtypeStruct((t, D_MODEL), F32),
        compiler_params=_params(1), name="mlp",
    )(x, g, wu, wd)


def _block_diag_means(seg_lens):
    m = np.zeros((MXU_DIM, MXU_DIM), np.float32)
    o = 0
    for n in seg_lens:
        m[o:o + n, o:o + n] = 1.0 / n
        o += n
    return jnp.asarray(m, BF16)


def _rope_tables(positions):
    pos = positions.reshape(-1).astype(F32)

    def tab(dim):
        inv = ROPE_THETA ** (-jnp.arange(0, dim, 2, dtype=F32) / dim)
        ang = pos[:, None] * inv
        return jnp.cos(ang), jnp.sin(ang)

    ch, sh = tab(A_HD)
    cr, sr = tab(B_ROPE)
    t = pos.shape[0]
    z = lambda n: jnp.zeros((t, n), F32)
    o = lambda n: jnp.ones((t, n), F32)
    cat = lambda *a: jnp.concatenate(a, axis=1)
    tabs_a = (cat(ch, ch, ch, ch), cat(-sh, z(32), -sh, z(32)), cat(z(32), sh, z(32), sh))
    tabs_b = (cat(o(64), cr, cr, o(32)), cat(z(64), -sr, z(48)), cat(z(80), sr, z(32)))
    return tabs_a, tabs_b


_A_PERM = np.array([(hh * 4 + p) * A_HD + d for p in range(4) for hh in range(2) for d in range(A_HD)])
_A_SINK_PERM = np.array([hh * 4 + p for p in range(4) for hh in range(2)])


def _split_w_in(w_in):
    sizes = ((A_HEADS * A_HD, A_KV_HEADS * A_HD, A_KV_HEADS * A_HD, B_Q_LORA, B_KV_LORA, B_ROPE)
             + (C_HEADS * C_HD,) * (3 * len(C_PATTERNS)) + (M_HEADS * M_HD, N_BRANCH * D_MODEL))
    offs = np.cumsum((0,) + sizes)
    return [w_in[..., offs[i]:offs[i + 1]] for i in range(len(sizes))]


def _pad_heads(w, widths, slot):
    per = sum(widths)
    heads = w.shape[-1] // per
    wh = w.reshape(w.shape[:-1] + (heads, per))
    pad = [(0, 0)] * (wh.ndim - 1) + [(0, slot - per)]
    return jnp.pad(wh, pad).reshape(w.shape[:-1] + (heads * slot,))


def kernel(x, mem, positions, g_mix, w_in, b_gate, a_qn, a_kn, a_sink, b_qa_norm, b_kva_norm,
           b_w_uq, b_w_ukv, b_qn, b_kn, c_qn, c_kn, m_g_mem, m_w_kv, m_qn, m_kn,
           w_branch, w_out, g_mlp, w_up, w_down):
    batch, seq, d = x.shape
    depth = g_mix.shape[0]
    t = batch * seq
    assert d == D_MODEL and seq % ROW_TILE == 0 and seq % (BLOCK * C_PATTERNS[-1][1]) == 0

    parts = _split_w_in(w_in)
    a_q, a_k, a_v, b_cq, b_ckv, b_kr = parts[:6]
    c_parts = parts[6:15]
    m_q, w_gate = parts[15], parts[16]
    w_a = jnp.concatenate([a_q[..., _A_PERM], a_k, a_v], axis=-1).astype(BF16)
    kr_slot = jnp.pad(b_kr, ((0, 0), (0, 0), (B_NOPE, LANES - B_NOPE - B_ROPE)))
    w_b = jnp.concatenate([b_cq, b_ckv, kr_slot], axis=-1).astype(BF16)
    w_uq = _pad_heads(b_w_uq, (B_NOPE, B_ROPE), LANES).astype(BF16)
    ukv = b_w_ukv.reshape(depth, B_KV_LORA, B_HEADS, B_NOPE + B_V)
    w_uk = _pad_heads(ukv[..., :B_NOPE].reshape(depth, B_KV_LORA, -1), (B_NOPE,), LANES).astype(BF16)
    w_uv = ukv[..., B_NOPE:].reshape(depth, B_KV_LORA, B_HEADS * B_V).astype(BF16)
    w_c = jnp.concatenate(c_parts, axis=-1).astype(BF16)
    w_mq = m_q.astype(BF16)
    w_gate = w_gate.astype(BF16)
    w_mkv = m_w_kv.astype(BF16)
    w_br = jnp.concatenate([w_branch[:, :1][:, :, _A_PERM], w_branch[:, 1:]], axis=1).astype(BF16)
    w_o = w_out.astype(BF16)
    w_u = w_up.astype(BF16)
    w_d = w_down.astype(BF16)

    row = lambda a: a.reshape(1, -1).astype(F32)
    seg64 = _block_diag_means((64,) * 4)
    seg_b = _block_diag_means((64, 32, 32) * 2)
    zeros32 = jnp.zeros((depth, LANES - B_NOPE - B_ROPE), F32)
    b_scale = (B_NOPE + B_ROPE) ** -0.5
    gq_b = jnp.tile(jnp.concatenate([b_qn * b_scale, zeros32], axis=-1), (1, B_HEADS))
    gk_b = jnp.tile(jnp.concatenate([b_kn, zeros32], axis=-1), (1, B_HEADS))
    gq_a = jnp.tile(a_qn * (A_HD ** -0.5), (1, A_HEADS))
    gk_a = jnp.tile(a_kn, (1, A_KV_HEADS))
    gq_c = jnp.tile(c_qn * (C_HD ** -0.5), (1, 1, C_HEADS))
    gk_c = jnp.tile(c_kn, (1, 1, C_HEADS))
    gq_m = jnp.tile(m_qn * (M_HD ** -0.5), (1, M_HEADS))
    gk_m = jnp.tile(m_kn, (1, M_HEADS))
    sinks = a_sink[:, _A_SINK_PERM].astype(F32)

    tabs_a, tabs_b = _rope_tables(positions)
    xs = x.reshape(t, D_MODEL)
    mem2d = mem.reshape(batch * N_MEM, D_MODEL)

    for l in range(depth):
        g = row(g_mix[l])
        qa, ka, va = _proj_a(xs, g, w_a[l], seg64, row(gq_a[l]), row(gk_a[l]), tabs_a)
        qb, kb, vb = _proj_b(xs, g, w_b[l], row(b_qa_norm[l]), row(b_kva_norm[l]), w_uq[l], w_uk[l],
                             w_uv[l], seg_b, row(gq_b[l]), row(gk_b[l]), tabs_b)
        qkv_c = _proj_c(xs, g, w_c[l], seg64, gq_c[l].astype(F32), gk_c[l].astype(F32), tabs_a)
        mk, mv = _mem_kv(mem2d, row(m_g_mem[l]), w_mkv[l], row(gk_m[l]))
        o_m = _proj_m(xs, g, w_mq[l], row(gq_m[l]), mk, mv, seq)

        (o_a,) = _band_attention(qa, ka, va, sinks[l], batch=batch, dil=1, max_dist=BLOCK - 1,
                                 want_lse=False)
        o_b = _mla_attention(qb, kb, vb, batch=batch)
        o_c, lse_c = [], []
        for grp, (win, dil) in enumerate(C_PATTERNS):
            o_g, lse_g = _band_attention(qkv_c[3 * grp], qkv_c[3 * grp + 1], qkv_c[3 * grp + 2], None,
                                         batch=batch, dil=dil, max_dist=win // dil, want_lse=True)
            o_c.append(o_g)
            lse_c.append(lse_g)

        xs = _merge(xs, g, w_gate[l], row(b_gate[l]), o_a, o_b, o_c, lse_c, o_m, w_br[l], w_o[l])
        xs = _mlp(xs, row(g_mlp[l]), w_u[l], w_d[l])
    return xs.reshape(batch, seq, D_MODEL)
```

```python
import functools

import numpy as np
import jax
import jax.numpy as jnp
from jax import lax
from jax.experimental import pallas as pl
from jax.experimental.pallas import tpu as pltpu

F32 = jnp.float32
BF16 = jnp.bfloat16

D_MODEL = 1024
N_MEM = 256
BLOCK = 128
ROPE_THETA = 10000.0
EPS = 1e-6
NEG = -1e30
LOG2_E = 1.4426950408889634

A_HEADS, A_KV_HEADS, A_HD = 8, 2, 64
B_HEADS, B_Q_LORA, B_KV_LORA, B_NOPE, B_ROPE, B_V = 8, 384, 256, 64, 32, 64
C_PATTERNS = ((128, 1), (512, 4), (2048, 16))
C_HEADS, C_HD = 8, 64
M_HEADS, M_HD = 4, 128
N_BRANCH = 4
BRANCH_W = 512
D_FF = 4 * D_MODEL

LANES = 128
MXU_DIM = 256
VMEM_LIMIT = 56 * 1024 * 1024
ROW_TILE = 512
MLA_TQ = 256
MLA_TK = 512


def _dot(a, b):
    return jnp.dot(a, b, preferred_element_type=F32)


def _dot_t(a, b):
    return lax.dot_general(a, b, (((1,), (1,)), ((), ())), preferred_element_type=F32)


def _resident(shape):
    nd = len(shape)
    return pl.BlockSpec(shape, lambda *_: (0,) * nd, pipeline_mode=pl.Buffered(1))


def _rows(width, tm=None):
    return pl.BlockSpec((tm or ROW_TILE, width), lambda i: (i, 0))


def _params(n_axes):
    return pltpu.CompilerParams(dimension_semantics=("arbitrary",) * n_axes,
                                vmem_limit_bytes=VMEM_LIMIT)


def _rms_bf16(xf, g):
    y = xf * lax.rsqrt(jnp.mean(xf * xf, axis=-1, keepdims=True) + EPS)
    return (y * g).astype(BF16)


def _seg_norm(z, seg, gain):
    sq = (z * z).astype(BF16)
    w = z.shape[1]
    if w <= MXU_DIM:
        ms = _dot(sq, seg[:w, :w])
    else:
        ms = jnp.concatenate([_dot(sq[:, c:c + MXU_DIM], seg) for c in range(0, w, MXU_DIM)], axis=1)
    return z * lax.rsqrt(ms + EPS) * gain


def _rope(z, c, s1, s2, half):
    w = z.shape[1]
    n = w // LANES

    def tile(t):
        return t if n == 1 else jnp.concatenate([t] * n, axis=1)

    return z * tile(c) + pltpu.roll(z, w - half, 1) * tile(s1) + pltpu.roll(z, half, 1) * tile(s2)


def _proj_a_kernel(x_ref, g_ref, w_ref, seg_ref, gq_ref, gk_ref, c_ref, s1_ref, s2_ref,
                   q_out, k_out, v_out):
    h = _rms_bf16(x_ref[...], g_ref[...])
    z = _dot(h, w_ref[...])
    c, s1, s2 = c_ref[...], s1_ref[...], s2_ref[...]
    seg = seg_ref[...]
    q = _rope(_seg_norm(z[:, :512], seg, gq_ref[...]), c, s1, s2, A_HD // 2)
    k = _rope(_seg_norm(z[:, 512:640], seg, gk_ref[...]), c, s1, s2, A_HD // 2)
    q_out[...] = q.astype(BF16)
    k_out[...] = k.astype(BF16)
    v_out[...] = z[:, 640:768].astype(BF16)


def _proj_a(x, g, w, seg, gq, gk, tabs):
    t = x.shape[0]
    return pl.pallas_call(
        _proj_a_kernel,
        grid=(t // ROW_TILE,),
        in_specs=[_rows(D_MODEL), _resident(g.shape), _resident(w.shape), _resident(seg.shape),
                  _resident(gq.shape), _resident(gk.shape), _rows(LANES), _rows(LANES), _rows(LANES)],
        out_specs=[_rows(512), _rows(128), _rows(128)],
        out_shape=[jax.ShapeDtypeStruct((t, 512), BF16), jax.ShapeDtypeStruct((t, 128), BF16),
                   jax.ShapeDtypeStruct((t, 128), BF16)],
        compiler_params=_params(1), name="proj_a",
    )(x, g, w, seg, gq, gk, *tabs)


def _proj_b_kernel(x_ref, g_ref, w_ref, gqa_ref, gkva_ref, wuq_ref, wuk_ref, wuv_ref, seg_ref,
                   gq_ref, gk_ref, c_ref, s1_ref, s2_ref, q_out, k_out, v_out):
    h = _rms_bf16(x_ref[...], g_ref[...])
    z = _dot(h, w_ref[...])
    cq = _rms_bf16(z[:, :B_Q_LORA], gqa_ref[...])
    ckv = _rms_bf16(z[:, B_Q_LORA:B_Q_LORA + B_KV_LORA], gkva_ref[...])
    kr = z[:, B_Q_LORA + B_KV_LORA:]
    q_up = _dot(cq, wuq_ref[...])
    k_up = _dot(ckv, wuk_ref[...]) + jnp.concatenate([kr] * B_HEADS, axis=1)
    v_out[...] = _dot(ckv, wuv_ref[...]).astype(BF16)
    c, s1, s2 = c_ref[...], s1_ref[...], s2_ref[...]
    seg = seg_ref[...]
    q = _rope(_seg_norm(q_up, seg, gq_ref[...]), c, s1, s2, B_ROPE // 2)
    k = _rope(_seg_norm(k_up, seg, gk_ref[...]), c, s1, s2, B_ROPE // 2)
    q_out[...] = q.astype(BF16)
    k_out[...] = k.astype(BF16)


def _proj_b(x, g, w, gqa, gkva, wuq, wuk, wuv, seg, gq, gk, tabs):
    t = x.shape[0]
    consts = (g, w, gqa, gkva, wuq, wuk, wuv, seg, gq, gk)
    return pl.pallas_call(
        _proj_b_kernel,
        grid=(t // ROW_TILE,),
        in_specs=[_rows(D_MODEL)] + [_resident(a.shape) for a in consts] + [_rows(LANES)] * 3,
        out_specs=[_rows(1024), _rows(1024), _rows(512)],
        out_shape=[jax.ShapeDtypeStruct((t, 1024), BF16), jax.ShapeDtypeStruct((t, 1024), BF16),
                   jax.ShapeDtypeStruct((t, 512), BF16)],
        compiler_params=_params(1), name="proj_b",
    )(x, *consts, *tabs)


def _proj_c_kernel(x_ref, g_ref, w_ref, seg_ref, gq_ref, gk_ref, c_ref, s1_ref, s2_ref, *outs):
    h = _rms_bf16(x_ref[...], g_ref[...])
    c, s1, s2 = c_ref[...], s1_ref[...], s2_ref[...]
    seg = seg_ref[...]
    for grp in range(len(C_PATTERNS)):
        z = _dot(h, w_ref[:, grp * 1536:(grp + 1) * 1536])
        q = _rope(_seg_norm(z[:, :512], seg, gq_ref[grp:grp + 1, :]), c, s1, s2, C_HD // 2)
        k = _rope(_seg_norm(z[:, 512:1024], seg, gk_ref[grp:grp + 1, :]), c, s1, s2, C_HD // 2)
        outs[3 * grp][...] = q.astype(BF16)
        outs[3 * grp + 1][...] = k.astype(BF16)
        outs[3 * grp + 2][...] = z[:, 1024:].astype(BF16)


def _proj_c(x, g, w, seg, gq, gk, tabs):
    t = x.shape[0]
    n_out = 3 * len(C_PATTERNS)
    return pl.pallas_call(
        _proj_c_kernel,
        grid=(t // ROW_TILE,),
        in_specs=[_rows(D_MODEL), _resident(g.shape), _resident(w.shape), _resident(seg.shape),
                  _resident(gq.shape), _resident(gk.shape), _rows(LANES), _rows(LANES), _rows(LANES)],
        out_specs=[_rows(512)] * n_out,
        out_shape=[jax.ShapeDtypeStruct((t, 512), BF16)] * n_out,
        compiler_params=_params(1), name="proj_c",
    )(x, g, w, seg, gq, gk, *tabs)


def _head_norm_128(z, gain):
    parts = []
    for h0 in range(0, z.shape[1], LANES):
        zh = z[:, h0:h0 + LANES]
        parts.append(zh * lax.rsqrt(jnp.mean(zh * zh, axis=-1, keepdims=True) + EPS))
    return jnp.concatenate(parts, axis=1) * gain


def _mem_kv_kernel(mem_ref, g_ref, w_ref, gk_ref, k_out, v_out):
    h = _rms_bf16(mem_ref[...], g_ref[...])
    z = _dot(h, w_ref[...])
    k_out[...] = _head_norm_128(z[:, :512], gk_ref[...]).astype(BF16)
    v_out[...] = z[:, 512:].astype(BF16)


def _mem_kv(mem2d, g, w, gk):
    rows = mem2d.shape[0]
    return pl.pallas_call(
        _mem_kv_kernel,
        grid=(rows // N_MEM,),
        in_specs=[_rows(D_MODEL, N_MEM), _resident(g.shape), _resident(w.shape), _resident(gk.shape)],
        out_specs=[_rows(512, N_MEM), _rows(512, N_MEM)],
        out_shape=[jax.ShapeDtypeStruct((rows, 512), BF16)] * 2,
        compiler_params=_params(1), name="mem_kv",
    )(mem2d, g, w, gk)


def _proj_m_kernel(x_ref, g_ref, w_ref, gq_ref, mk_ref, mv_ref, o_out):
    h = _rms_bf16(x_ref[...], g_ref[...])
    q = _head_norm_128(_dot(h, w_ref[...]), gq_ref[...]).astype(BF16)
    for hd in range(M_HEADS):
        cols = slice(hd * M_HD, (hd + 1) * M_HD)
        s = _dot_t(q[:, cols], mk_ref[:, cols])
        m = jnp.max(s, axis=-1, keepdims=True)
        e = jnp.exp(s - m)
        l = jnp.sum(e, axis=-1, keepdims=True)
        o_out[:, cols] = (_dot(e.astype(BF16), mv_ref[:, cols]) * (1.0 / l)).astype(BF16)


def _proj_m(x, g, w, gq, mk, mv, seq):
    t = x.shape[0]
    per_batch = seq // ROW_TILE
    mem_spec = pl.BlockSpec((N_MEM, 512), lambda i: (i // per_batch, 0))
    return pl.pallas_call(
        _proj_m_kernel,
        grid=(t // ROW_TILE,),
        in_specs=[_rows(D_MODEL), _resident(g.shape), _resident(w.shape), _resident(gq.shape),
                  mem_spec, mem_spec],
        out_specs=_rows(512),
        out_shape=jax.ShapeDtypeStruct((t, 512), BF16),
        compiler_params=_params(1), name="proj_m",
    )(x, g, w, gq, mk, mv)


def _band_kernel(*refs, n_sub, kw, max_dist, use_sink, want_lse):
    refs = list(refs)
    sink_ref = refs.pop(0) if use_sink else None
    q_ref, kc_ref, kp_ref, vc_ref, vp_ref, o_out = refs[:6]
    lse_out = refs[6] if want_lse else None
    qi = lax.broadcasted_iota(jnp.int32, (BLOCK, 2 * BLOCK), 0)
    kj = lax.broadcasted_iota(jnp.int32, (BLOCK, 2 * BLOCK), 1)
    dist = qi - kj + BLOCK
    band = (dist >= 0) & (dist <= max_dist)
    prev_ok = jnp.where(pl.program_id(2) > 0, BLOCK, 0)
    first_mask = band & (kj + prev_ok >= BLOCK)
    lo = lax.broadcasted_iota(jnp.int32, (BLOCK, LANES), 1) < 64
    for j in range(n_sub):
        rows = slice(j * BLOCK, (j + 1) * BLOCK)
        prev = slice((j - 1) * BLOCK, j * BLOCK)
        mask = first_mask if j == 0 else band
        for p in range(4):
            kcol = slice(0, LANES) if kw == LANES else slice(p * LANES, (p + 1) * LANES)
            k_prev = kp_ref[:, kcol] if j == 0 else kc_ref[prev, kcol]
            v_prev = vp_ref[:, kcol] if j == 0 else vc_ref[prev, kcol]
            kk = jnp.concatenate([k_prev, kc_ref[rows, kcol]], axis=0)
            vv = jnp.concatenate([v_prev, vc_ref[rows, kcol]], axis=0)
            qp = q_ref[rows, p * LANES:(p + 1) * LANES]
            outs, lses = [], []
            for hh in range(2):
                qm = jnp.where(lo if hh == 0 else jnp.logical_not(lo), qp, jnp.zeros_like(qp))
                s = jnp.where(mask, _dot_t(qm, kk), NEG)
                m = jnp.max(s, axis=-1, keepdims=True)
                if use_sink:
                    sk = sink_ref[2 * p + hh]
                    m = jnp.maximum(m, sk)
                e = jnp.exp(s - m)
                l = jnp.sum(e, axis=-1, keepdims=True)
                if use_sink:
                    l = l + jnp.exp(sk - m)
                outs.append(_dot(e.astype(BF16), vv) * (1.0 / l))
                lses.append(m + jnp.log(l))
            o_out[rows, p * LANES:(p + 1) * LANES] = jnp.where(lo, outs[0], outs[1]).astype(BF16)
            if want_lse:
                lse_out[rows, p * LANES:(p + 1) * LANES] = jnp.where(
                    lo, jnp.broadcast_to(lses[0], (BLOCK, LANES)), jnp.broadcast_to(lses[1], (BLOCK, LANES)))


def _band_attention(q, k, v, sinks, *, batch, dil, max_dist, want_lse):
    t, kw = k.shape
    seq = t // batch
    length = seq // dil
    tq = min(ROW_TILE, length)
    n_sub = tq // BLOCK
    qv = q.reshape(batch, length, dil * 512)
    kv = k.reshape(batch, length, dil * kw)
    vv = v.reshape(batch, length, dil * kw)
    cur = lambda w: pl.BlockSpec((None, tq, w), lambda b, r, i: (b, i, r))
    prv = pl.BlockSpec((None, BLOCK, kw), lambda b, r, i: (b, jnp.maximum(i * n_sub - 1, 0), r))
    use_sink = sinks is not None
    in_specs = [cur(512), cur(kw), prv, cur(kw), prv]
    args = [qv, kv, kv, vv, vv]
    if use_sink:
        in_specs = [pl.BlockSpec(memory_space=pltpu.SMEM)] + in_specs
        args = [sinks] + args
    out_specs = [cur(512)]
    out_shape = [jax.ShapeDtypeStruct((batch, length, dil * 512), BF16)]
    if want_lse:
        out_specs.append(cur(512))
        out_shape.append(jax.ShapeDtypeStruct((batch, length, dil * 512), F32))
    res = pl.pallas_call(
        functools.partial(_band_kernel, n_sub=n_sub, kw=kw, max_dist=max_dist, use_sink=use_sink,
                          want_lse=want_lse),
        grid=(batch, dil, length // tq),
        in_specs=in_specs, out_specs=out_specs, out_shape=out_shape,
        compiler_params=_params(3), name="band_attention_d%d" % dil,
    )(*args)
    return [r.reshape(t, 512) for r in res]


def _mla_kernel(q_ref, k_ref, v_ref, o_out, m_s, l_s, acc_s, *, tq, tk):
    i = pl.program_id(1)
    n_full = (i * tq) // tk
    row = lax.broadcasted_iota(jnp.int32, (tq, tk), 0) + i * tq
    col = lax.broadcasted_iota(jnp.int32, (tq, tk), 1) + n_full * tk
    causal = col <= row
    lo = lax.broadcasted_iota(jnp.int32, (tq, LANES), 1) < 64
    lo_k = lax.broadcasted_iota(jnp.int32, (tk, LANES), 1) < 64
    m_s[...] = jnp.full(m_s.shape, NEG, F32)
    l_s[...] = jnp.zeros(l_s.shape, F32)
    acc_s[...] = jnp.zeros(acc_s.shape, F32)

    def step(j, masked):
        start = pl.multiple_of(j * tk, tk)
        for p in range(B_HEADS // 2):
            vt = v_ref[pl.ds(start, tk), p * LANES:(p + 1) * LANES]
            zero = jnp.zeros_like(vt)
            v_cat = jnp.concatenate([jnp.where(lo_k, vt, zero), jnp.where(lo_k, zero, vt)], axis=0)
            probs, alphas = [], []
            for hh in range(2):
                h = 2 * p + hh
                hcols = slice(h * LANES, (h + 1) * LANES)
                s = _dot_t(q_ref[:, hcols], k_ref[pl.ds(start, tk), hcols])
                if masked:
                    s = jnp.where(causal, s, NEG)
                chunks = [s[:, c:c + LANES] for c in range(0, tk, LANES)]
                m_old = m_s[h]
                m_new = jnp.maximum(m_old, jnp.max(functools.reduce(jnp.maximum, chunks), axis=-1, keepdims=True))
                alpha = jnp.exp2(m_old - m_new)
                es = [jnp.exp2(ch - m_new) for ch in chunks]
                l_s[h] = alpha * l_s[h] + jnp.sum(functools.reduce(jnp.add, es), axis=-1, keepdims=True)
                m_s[h] = m_new
                probs.extend(e.astype(BF16) for e in es)
                alphas.append(alpha)
            pv = _dot(jnp.concatenate(probs, axis=1), v_cat)
            acc_s[p] = jnp.where(lo, alphas[0], alphas[1]) * acc_s[p] + pv

    def body(j, carry):
        step(j, False)
        return carry

    lax.fori_loop(0, n_full, body, 0)
    step(n_full, True)
    for p in range(B_HEADS // 2):
        inv = jnp.where(lo, 1.0 / l_s[2 * p], 1.0 / l_s[2 * p + 1])
        o_out[:, p * LANES:(p + 1) * LANES] = (acc_s[p] * inv).astype(BF16)


def _mla_attention(q, k, v, *, batch):
    t = q.shape[0]
    seq = t // batch
    tq = min(MLA_TQ, seq)
    tk = min(MLA_TK, seq)
    assert tk % tq == 0 and seq % tk == 0
    q3, k3, v3 = (a.reshape(batch, seq, a.shape[1]) for a in (q, k, v))
    o = pl.pallas_call(
        functools.partial(_mla_kernel, tq=tq, tk=tk),
        grid=(batch, seq // tq),
        in_specs=[pl.BlockSpec((None, tq, 1024), lambda b, i: (b, i, 0)),
                  pl.BlockSpec((None, seq, 1024), lambda b, i: (b, 0, 0)),
                  pl.BlockSpec((None, seq, 512), lambda b, i: (b, 0, 0))],
        out_specs=pl.BlockSpec((None, tq, 512), lambda b, i: (b, i, 0)),
        out_shape=jax.ShapeDtypeStruct((batch, seq, 512), BF16),
        scratch_shapes=[pltpu.VMEM((B_HEADS, tq, LANES), F32), pltpu.VMEM((B_HEADS, tq, LANES), F32),
                        pltpu.VMEM((B_HEADS // 2, tq, LANES), F32)],
        compiler_params=_params(2), name="mla_attention",
    )(q3, k3, v3)
    return o.reshape(t, 512)


def _merge_kernel(x_ref, g_ref, wg_ref, bg_ref, oa_ref, ob_ref, oc0_ref, oc1_ref, oc2_ref,
                  l0_ref, l1_ref, l2_ref, om_ref, wb_ref, wo_ref, x_out):
    xf = x_ref[...]
    h = _rms_bf16(xf, g_ref[...])
    l0, l1, l2 = l0_ref[...], l1_ref[...], l2_ref[...]
    mx = jnp.maximum(jnp.maximum(l0, l1), l2)
    e0, e1, e2 = jnp.exp(l0 - mx), jnp.exp(l1 - mx), jnp.exp(l2 - mx)
    o_c = (e0 * oc0_ref[...].astype(F32) + e1 * oc1_ref[...].astype(F32)
           + e2 * oc2_ref[...].astype(F32)) * (1.0 / (e0 + e1 + e2))
    branches = (oa_ref[...], ob_ref[...], o_c.astype(BF16), om_ref[...])
    acc = jnp.zeros(xf.shape, F32)
    for n in range(N_BRANCH):
        cols = slice(n * D_MODEL, (n + 1) * D_MODEL)
        gate = jax.nn.sigmoid(_dot(h, wg_ref[:, cols]) + bg_ref[:, cols])
        acc = acc + gate * _dot(branches[n], wb_ref[n])
    x_out[...] = xf + _dot(acc.astype(BF16), wo_ref[...])


def _merge(x, g, wg, bg, oa, ob, oc, lse, om, wb, wo):
    t = x.shape[0]
    return pl.pallas_call(
        _merge_kernel,
        grid=(t // ROW_TILE,),
        in_specs=[_rows(D_MODEL), _resident(g.shape), _resident(wg.shape), _resident(bg.shape)]
                 + [_rows(512)] * 9 + [_resident(wb.shape), _resident(wo.shape)],
        out_specs=_rows(D_MODEL),
        out_shape=jax.ShapeDtypeStruct((t, D_MODEL), F32),
        compiler_params=_params(1), name="merge",
    )(x, g, wg, bg, oa, ob, *oc, *lse, om, wb, wo)


def _mlp_kernel(x_ref, g_ref, wu_ref, wd_ref, x_out):
    xf = x_ref[...]
    h = _rms_bf16(xf, g_ref[...])
    acc = xf
    for c0 in range(0, D_FF, D_MODEL):
        u = jnp.maximum(_dot(h, wu_ref[:, c0:c0 + D_MODEL]), 0.0)
        acc = acc + _dot((u * u).astype(BF16), wd_ref[c0:c0 + D_MODEL, :])
    x_out[...] = acc


def _mlp(x, g, wu, wd):
    t = x.shape[0]
    return pl.pallas_call(
        _mlp_kernel,
        grid=(t // ROW_TILE,),
        in_specs=[_rows(D_MODEL), _resident(g.shape), _resident(wu.shape), _resident(wd.shape)],
        out_specs=_rows(D_MODEL),
        out_shape=jax.ShapeDtypeStruct((t, D_MODEL), F32),
        compiler_params=_params(1), name="mlp",
    )(x, g, wu, wd)


def _block_diag_means(seg_lens):
    m = np.zeros((MXU_DIM, MXU_DIM), np.float32)
    o = 0
    for n in seg_lens:
        m[o:o + n, o:o + n] = 1.0 / n
        o += n
    return jnp.asarray(m, BF16)


def _rope_tables(positions):
    pos = positions.reshape(-1).astype(F32)

    def tab(dim):
        inv = ROPE_THETA ** (-jnp.arange(0, dim, 2, dtype=F32) / dim)
        ang = pos[:, None] * inv
        return jnp.cos(ang), jnp.sin(ang)

    ch, sh = tab(A_HD)
    cr, sr = tab(B_ROPE)
    t = pos.shape[0]
    z = lambda n: jnp.zeros((t, n), F32)
    o = lambda n: jnp.ones((t, n), F32)
    cat = lambda *a: jnp.concatenate(a, axis=1)
    tabs_a = (cat(ch, ch, ch, ch), cat(-sh, z(32), -sh, z(32)), cat(z(32), sh, z(32), sh))
    tabs_b = (cat(o(64), cr, cr, o(32)), cat(z(64), -sr, z(48)), cat(z(80), sr, z(32)))
    return tabs_a, tabs_b


_A_PERM = np.array([(hh * 4 + p) * A_HD + d for p in range(4) for hh in range(2) for d in range(A_HD)])
_A_SINK_PERM = np.array([hh * 4 + p for p in range(4) for hh in range(2)])


def _split_w_in(w_in):
    sizes = ((A_HEADS * A_HD, A_KV_HEADS * A_HD, A_KV_HEADS * A_HD, B_Q_LORA, B_KV_LORA, B_ROPE)
             + (C_HEADS * C_HD,) * (3 * len(C_PATTERNS)) + (M_HEADS * M_HD, N_BRANCH * D_MODEL))
    offs = np.cumsum((0,) + sizes)
    return [w_in[..., offs[i]:offs[i + 1]] for i in range(len(sizes))]


def _pad_heads(w, widths, slot):
    per = sum(widths)
    heads = w.shape[-1] // per
    wh = w.reshape(w.shape[:-1] + (heads, per))
    pad = [(0, 0)] * (wh.ndim - 1) + [(0, slot - per)]
    return jnp.pad(wh, pad).reshape(w.shape[:-1] + (heads * slot,))


def kernel(x, mem, positions, g_mix, w_in, b_gate, a_qn, a_kn, a_sink, b_qa_norm, b_kva_norm,
           b_w_uq, b_w_ukv, b_qn, b_kn, c_qn, c_kn, m_g_mem, m_w_kv, m_qn, m_kn,
           w_branch, w_out, g_mlp, w_up, w_down):
    batch, seq, d = x.shape
    depth = g_mix.shape[0]
    t = batch * seq
    assert d == D_MODEL and seq % ROW_TILE == 0 and seq % (BLOCK * C_PATTERNS[-1][1]) == 0

    parts = _split_w_in(w_in)
    a_q, a_k, a_v, b_cq, b_ckv, b_kr = parts[:6]
    c_parts = parts[6:15]
    m_q, w_gate = parts[15], parts[16]
    w_a = jnp.concatenate([a_q[..., _A_PERM], a_k, a_v], axis=-1).astype(BF16)
    kr_slot = jnp.pad(b_kr, ((0, 0), (0, 0), (B_NOPE, LANES - B_NOPE - B_ROPE)))
    w_b = jnp.concatenate([b_cq, b_ckv, kr_slot], axis=-1).astype(BF16)
    w_uq = _pad_heads(b_w_uq, (B_NOPE, B_ROPE), LANES).astype(BF16)
    ukv = b_w_ukv.reshape(depth, B_KV_LORA, B_HEADS, B_NOPE + B_V)
    w_uk = _pad_heads(ukv[..., :B_NOPE].reshape(depth, B_KV_LORA, -1), (B_NOPE,), LANES).astype(BF16)
    w_uv = ukv[..., B_NOPE:].reshape(depth, B_KV_LORA, B_HEADS * B_V).astype(BF16)
    w_c = jnp.concatenate(c_parts, axis=-1).astype(BF16)
    w_mq = m_q.astype(BF16)
    w_gate = w_gate.astype(BF16)
    w_mkv = m_w_kv.astype(BF16)
    w_br = jnp.concatenate([w_branch[:, :1][:, :, _A_PERM], w_branch[:, 1:]], axis=1).astype(BF16)
    w_o = w_out.astype(BF16)
    w_u = w_up.astype(BF16)
    w_d = w_down.astype(BF16)

    row = lambda a: a.reshape(1, -1).astype(F32)
    seg64 = _block_diag_means((64,) * 4)
    seg_b = _block_diag_means((64, 32, 32) * 2)
    zeros32 = jnp.zeros((depth, LANES - B_NOPE - B_ROPE), F32)
    b_scale = (B_NOPE + B_ROPE) ** -0.5 * LOG2_E
    gq_b = jnp.tile(jnp.concatenate([b_qn * b_scale, zeros32], axis=-1), (1, B_HEADS))
    gk_b = jnp.tile(jnp.concatenate([b_kn, zeros32], axis=-1), (1, B_HEADS))
    gq_a = jnp.tile(a_qn * (A_HD ** -0.5), (1, A_HEADS))
    gk_a = jnp.tile(a_kn, (1, A_KV_HEADS))
    gq_c = jnp.tile(c_qn * (C_HD ** -0.5), (1, 1, C_HEADS))
    gk_c = jnp.tile(c_kn, (1, 1, C_HEADS))
    gq_m = jnp.tile(m_qn * (M_HD ** -0.5), (1, M_HEADS))
    gk_m = jnp.tile(m_kn, (1, M_HEADS))
    sinks = a_sink[:, _A_SINK_PERM].astype(F32)

    tabs_a, tabs_b = _rope_tables(positions)
    xs = x.reshape(t, D_MODEL)
    mem2d = mem.reshape(batch * N_MEM, D_MODEL)

    for l in range(depth):
        g = row(g_mix[l])
        qa, ka, va = _proj_a(xs, g, w_a[l], seg64, row(gq_a[l]), row(gk_a[l]), tabs_a)
        qb, kb, vb = _proj_b(xs, g, w_b[l], row(b_qa_norm[l]), row(b_kva_norm[l]), w_uq[l], w_uk[l],
                             w_uv[l], seg_b, row(gq_b[l]), row(gk_b[l]), tabs_b)
        qkv_c = _proj_c(xs, g, w_c[l], seg64, gq_c[l].astype(F32), gk_c[l].astype(F32), tabs_a)
        mk, mv = _mem_kv(mem2d, row(m_g_mem[l]), w_mkv[l], row(gk_m[l]))
        o_m = _proj_m(xs, g, w_mq[l], row(gq_m[l]), mk, mv, seq)

        (o_a,) = _band_attention(qa, ka, va, sinks[l], batch=batch, dil=1, max_dist=BLOCK - 1,
                                 want_lse=False)
        o_b = _mla_attention(qb, kb, vb, batch=batch)
        o_c, lse_c = [], []
        for grp, (win, dil) in enumerate(C_PATTERNS):
            o_g, lse_g = _band_attention(qkv_c[3 * grp], qkv_c[3 * grp + 1], qkv_c[3 * grp + 2], None,
                                         batch=batch, dil=dil, max_dist=win // dil, want_lse=True)
            o_c.append(o_g)
            lse_c.append(lse_g)

        xs = _merge(xs, g, w_gate[l], row(b_gate[l]), o_a, o_b, o_c, lse_c, o_m, w_br[l], w_o[l])
        xs = _mlp(xs, row(g_mlp[l]), w_u[l], w_d[l])
    return xs.reshape(batch, seq, D_MODEL)
```

```python
import functools

import numpy as np
import jax
import jax.numpy as jnp
from jax import lax
from jax.experimental import pallas as pl
from jax.experimental.pallas import tpu as pltpu

F32 = jnp.float32
BF16 = jnp.bfloat16

D_MODEL = 1024
N_MEM = 256
BLOCK = 128
ROPE_THETA = 10000.0
EPS = 1e-6
NEG = -1e30
LOG2_E = 1.4426950408889634

A_HEADS, A_KV_HEADS, A_HD = 8, 2, 64
B_HEADS, B_Q_LORA, B_KV_LORA, B_NOPE, B_ROPE, B_V = 8, 384, 256, 64, 32, 64
C_PATTERNS = ((128, 1), (512, 4), (2048, 16))
C_HEADS, C_HD = 8, 64
M_HEADS, M_HD = 4, 128
N_BRANCH = 4
BRANCH_W = 512
D_FF = 4 * D_MODEL

LANES = 128
MXU_DIM = 256
VMEM_LIMIT = 56 * 1024 * 1024
ROW_TILE = 512
MLA_TQ = 256
MLA_TK = 512


def _dot(a, b):
    return jnp.dot(a, b, preferred_element_type=F32)


def _dot_t(a, b):
    return lax.dot_general(a, b, (((1,), (1,)), ((), ())), preferred_element_type=F32)


def _resident(shape):
    nd = len(shape)
    return pl.BlockSpec(shape, lambda *_: (0,) * nd, pipeline_mode=pl.Buffered(1))


def _rows(width, tm=None):
    return pl.BlockSpec((tm or ROW_TILE, width), lambda i: (i, 0))


def _params(n_axes):
    return pltpu.CompilerParams(dimension_semantics=("arbitrary",) * n_axes,
                                vmem_limit_bytes=VMEM_LIMIT)


def _rms_bf16(xf, g):
    y = xf * lax.rsqrt(jnp.mean(xf * xf, axis=-1, keepdims=True) + EPS)
    return (y * g).astype(BF16)


def _seg_norm(z, seg, gain):
    sq = (z * z).astype(BF16)
    w = z.shape[1]
    if w <= MXU_DIM:
        ms = _dot(sq, seg[:w, :w])
    else:
        ms = jnp.concatenate([_dot(sq[:, c:c + MXU_DIM], seg) for c in range(0, w, MXU_DIM)], axis=1)
    return z * lax.rsqrt(ms + EPS) * gain


def _rope(z, c, s1, s2, half):
    w = z.shape[1]
    n = w // LANES

    def tile(t):
        return t if n == 1 else jnp.concatenate([t] * n, axis=1)

    return z * tile(c) + pltpu.roll(z, w - half, 1) * tile(s1) + pltpu.roll(z, half, 1) * tile(s2)


def _proj_a_kernel(x_ref, g_ref, w_ref, seg_ref, gq_ref, gk_ref, c_ref, s1_ref, s2_ref,
                   q_out, k_out, v_out):
    h = _rms_bf16(x_ref[...], g_ref[...])
    z = _dot(h, w_ref[...])
    c, s1, s2 = c_ref[...], s1_ref[...], s2_ref[...]
    seg = seg_ref[...]
    q = _rope(_seg_norm(z[:, :512], seg, gq_ref[...]), c, s1, s2, A_HD // 2)
    k = _rope(_seg_norm(z[:, 512:640], seg, gk_ref[...]), c, s1, s2, A_HD // 2)
    q_out[...] = q.astype(BF16)
    k_out[...] = k.astype(BF16)
    v_out[...] = z[:, 640:768].astype(BF16)


def _proj_a(x, g, w, seg, gq, gk, tabs):
    t = x.shape[0]
    return pl.pallas_call(
        _proj_a_kernel,
        grid=(t // ROW_TILE,),
        in_specs=[_rows(D_MODEL), _resident(g.shape), _resident(w.shape), _resident(seg.shape),
                  _resident(gq.shape), _resident(gk.shape), _rows(LANES), _rows(LANES), _rows(LANES)],
        out_specs=[_rows(512), _rows(128), _rows(128)],
        out_shape=[jax.ShapeDtypeStruct((t, 512), BF16), jax.ShapeDtypeStruct((t, 128), BF16),
                   jax.ShapeDtypeStruct((t, 128), BF16)],
        compiler_params=_params(1), name="proj_a",
    )(x, g, w, seg, gq, gk, *tabs)


def _proj_b_kernel(x_ref, g_ref, w_ref, gqa_ref, gkva_ref, wuq_ref, wuk_ref, wuv_ref, seg_ref,
                   gq_ref, gk_ref, c_ref, s1_ref, s2_ref, q_out, k_out, v_out):
    h = _rms_bf16(x_ref[...], g_ref[...])
    z = _dot(h, w_ref[...])
    cq = _rms_bf16(z[:, :B_Q_LORA], gqa_ref[...])
    ckv = _rms_bf16(z[:, B_Q_LORA:B_Q_LORA + B_KV_LORA], gkva_ref[...])
    kr = z[:, B_Q_LORA + B_KV_LORA:]
    q_up = _dot(cq, wuq_ref[...])
    k_up = _dot(ckv, wuk_ref[...]) + jnp.concatenate([kr] * B_HEADS, axis=1)
    v_out[...] = _dot(ckv, wuv_ref[...]).astype(BF16)
    c, s1, s2 = c_ref[...], s1_ref[...], s2_ref[...]
    seg = seg_ref[...]
    q = _rope(_seg_norm(q_up, seg, gq_ref[...]), c, s1, s2, B_ROPE // 2)
    k = _rope(_seg_norm(k_up, seg, gk_ref[...]), c, s1, s2, B_ROPE // 2)
    q_out[...] = q.astype(BF16)
    k_out[...] = k.astype(BF16)


def _proj_b(x, g, w, gqa, gkva, wuq, wuk, wuv, seg, gq, gk, tabs):
    t = x.shape[0]
    consts = (g, w, gqa, gkva, wuq, wuk, wuv, seg, gq, gk)
    return pl.pallas_call(
        _proj_b_kernel,
        grid=(t // ROW_TILE,),
        in_specs=[_rows(D_MODEL)] + [_resident(a.shape) for a in consts] + [_rows(LANES)] * 3,
        out_specs=[_rows(1024), _rows(1024), _rows(512)],
        out_shape=[jax.ShapeDtypeStruct((t, 1024), BF16), jax.ShapeDtypeStruct((t, 1024), BF16),
                   jax.ShapeDtypeStruct((t, 512), BF16)],
        compiler_params=_params(1), name="proj_b",
    )(x, *consts, *tabs)


def _store_gathered(out_ref, val, slab, dil):
    if dil == 1:
        out_ref[...] = val.astype(BF16)
        return
    n = val.shape[0] // dil
    for c in range(4):
        slab[c] = val[:, c * LANES:(c + 1) * LANES]
    for r in range(dil):
        for c in range(4):
            lane0 = r * 512 + c * LANES
            out_ref[:, lane0:lane0 + LANES] = slab[c, pl.ds(r, n, stride=dil), :].astype(BF16)


def _load_gathered(in_ref, slab, dil):
    if dil == 1:
        return in_ref[...].astype(F32)
    n = in_ref.shape[0]
    for r in range(dil):
        for c in range(4):
            lane0 = r * 512 + c * LANES
            slab[c, pl.ds(r, n, stride=dil), :] = in_ref[:, lane0:lane0 + LANES].astype(F32)
    return jnp.concatenate([slab[c] for c in range(4)], axis=1)


def _proj_c_kernel(x_ref, g_ref, w_ref, seg_ref, gq_ref, gk_ref, c_ref, s1_ref, s2_ref, *refs):
    outs, slabs = refs[:9], refs[9:]
    h = _rms_bf16(x_ref[...], g_ref[...])
    c, s1, s2 = c_ref[...], s1_ref[...], s2_ref[...]
    seg = seg_ref[...]
    for grp, (_, dil) in enumerate(C_PATTERNS):
        z = _dot(h, w_ref[:, grp * 1536:(grp + 1) * 1536])
        q = _rope(_seg_norm(z[:, :512], seg, gq_ref[grp:grp + 1, :]), c, s1, s2, C_HD // 2)
        k = _rope(_seg_norm(z[:, 512:1024], seg, gk_ref[grp:grp + 1, :]), c, s1, s2, C_HD // 2)
        _store_gathered(outs[3 * grp], q, slabs[0], dil)
        _store_gathered(outs[3 * grp + 1], k, slabs[1], dil)
        _store_gathered(outs[3 * grp + 2], z[:, 1024:], slabs[2], dil)


def _proj_c(x, g, w, seg, gq, gk, tabs):
    t = x.shape[0]
    out_specs, out_shape = [], []
    for _, dil in C_PATTERNS:
        out_specs += [_rows(dil * 512, ROW_TILE // dil)] * 3
        out_shape += [jax.ShapeDtypeStruct((t // dil, dil * 512), BF16)] * 3
    return pl.pallas_call(
        _proj_c_kernel,
        grid=(t // ROW_TILE,),
        in_specs=[_rows(D_MODEL), _resident(g.shape), _resident(w.shape), _resident(seg.shape),
                  _resident(gq.shape), _resident(gk.shape), _rows(LANES), _rows(LANES), _rows(LANES)],
        out_specs=out_specs, out_shape=out_shape,
        scratch_shapes=[pltpu.VMEM((4, ROW_TILE, LANES), F32)] * 3,
        compiler_params=_params(1), name="proj_c",
    )(x, g, w, seg, gq, gk, *tabs)


def _head_norm_128(z, gain):
    parts = []
    for h0 in range(0, z.shape[1], LANES):
        zh = z[:, h0:h0 + LANES]
        parts.append(zh * lax.rsqrt(jnp.mean(zh * zh, axis=-1, keepdims=True) + EPS))
    return jnp.concatenate(parts, axis=1) * gain


def _mem_kv_kernel(mem_ref, g_ref, w_ref, gk_ref, k_out, v_out):
    h = _rms_bf16(mem_ref[...], g_ref[...])
    z = _dot(h, w_ref[...])
    k_out[...] = _head_norm_128(z[:, :512], gk_ref[...]).astype(BF16)
    v_out[...] = z[:, 512:].astype(BF16)


def _mem_kv(mem2d, g, w, gk):
    rows = mem2d.shape[0]
    return pl.pallas_call(
        _mem_kv_kernel,
        grid=(rows // N_MEM,),
        in_specs=[_rows(D_MODEL, N_MEM), _resident(g.shape), _resident(w.shape), _resident(gk.shape)],
        out_specs=[_rows(512, N_MEM), _rows(512, N_MEM)],
        out_shape=[jax.ShapeDtypeStruct((rows, 512), BF16)] * 2,
        compiler_params=_params(1), name="mem_kv",
    )(mem2d, g, w, gk)


def _proj_m_kernel(x_ref, g_ref, w_ref, gq_ref, mk_ref, mv_ref, o_out):
    h = _rms_bf16(x_ref[...], g_ref[...])
    q = _head_norm_128(_dot(h, w_ref[...]), gq_ref[...]).astype(BF16)
    for hd in range(M_HEADS):
        cols = slice(hd * M_HD, (hd + 1) * M_HD)
        s = _dot_t(q[:, cols], mk_ref[:, cols])
        m = jnp.max(s, axis=-1, keepdims=True)
        e = jnp.exp(s - m)
        l = jnp.sum(e, axis=-1, keepdims=True)
        o_out[:, cols] = (_dot(e.astype(BF16), mv_ref[:, cols]) * (1.0 / l)).astype(BF16)


def _proj_m(x, g, w, gq, mk, mv, seq):
    t = x.shape[0]
    per_batch = seq // ROW_TILE
    mem_spec = pl.BlockSpec((N_MEM, 512), lambda i: (i // per_batch, 0))
    return pl.pallas_call(
        _proj_m_kernel,
        grid=(t // ROW_TILE,),
        in_specs=[_rows(D_MODEL), _resident(g.shape), _resident(w.shape), _resident(gq.shape),
                  mem_spec, mem_spec],
        out_specs=_rows(512),
        out_shape=jax.ShapeDtypeStruct((t, 512), BF16),
        compiler_params=_params(1), name="proj_m",
    )(x, g, w, gq, mk, mv)


def _band_kernel(*refs, n_sub, kw, max_dist, use_sink, want_lse):
    refs = list(refs)
    sink_ref = refs.pop(0) if use_sink else None
    q_ref, kc_ref, kp_ref, vc_ref, vp_ref, o_out = refs[:6]
    lse_out = refs[6] if want_lse else None
    qi = lax.broadcasted_iota(jnp.int32, (BLOCK, 2 * BLOCK), 0)
    kj = lax.broadcasted_iota(jnp.int32, (BLOCK, 2 * BLOCK), 1)
    dist = qi - kj + BLOCK
    band = (dist >= 0) & (dist <= max_dist)
    prev_ok = jnp.where(pl.program_id(2) > 0, BLOCK, 0)
    first_mask = band & (kj + prev_ok >= BLOCK)
    lo = lax.broadcasted_iota(jnp.int32, (BLOCK, LANES), 1) < 64
    for j in range(n_sub):
        rows = slice(j * BLOCK, (j + 1) * BLOCK)
        prev = slice((j - 1) * BLOCK, j * BLOCK)
        mask = first_mask if j == 0 else band
        for p in range(4):
            kcol = slice(0, LANES) if kw == LANES else slice(p * LANES, (p + 1) * LANES)
            k_prev = kp_ref[:, kcol] if j == 0 else kc_ref[prev, kcol]
            v_prev = vp_ref[:, kcol] if j == 0 else vc_ref[prev, kcol]
            kk = jnp.concatenate([k_prev, kc_ref[rows, kcol]], axis=0)
            vv = jnp.concatenate([v_prev, vc_ref[rows, kcol]], axis=0)
            qp = q_ref[rows, p * LANES:(p + 1) * LANES]
            outs, lses = [], []
            for hh in range(2):
                qm = jnp.where(lo if hh == 0 else jnp.logical_not(lo), qp, jnp.zeros_like(qp))
                s = jnp.where(mask, _dot_t(qm, kk), NEG)
                m = jnp.max(s, axis=-1, keepdims=True)
                if use_sink:
                    sk = sink_ref[2 * p + hh]
                    m = jnp.maximum(m, sk)
                e = jnp.exp(s - m)
                l = jnp.sum(e, axis=-1, keepdims=True)
                if use_sink:
                    l = l + jnp.exp(sk - m)
                outs.append(_dot(e.astype(BF16), vv) * (1.0 / l))
                lses.append(m + jnp.log(l))
            o_out[rows, p * LANES:(p + 1) * LANES] = jnp.where(lo, outs[0], outs[1]).astype(BF16)
            if want_lse:
                lse_out[rows, p * LANES:(p + 1) * LANES] = jnp.where(
                    lo, jnp.broadcast_to(lses[0], (BLOCK, LANES)), jnp.broadcast_to(lses[1], (BLOCK, LANES)))


def _band_attention(q, k, v, sinks, *, batch, dil, max_dist, want_lse):
    kw = k.shape[1] // dil
    length = q.shape[0] // batch
    tq = min(ROW_TILE, length)
    n_sub = tq // BLOCK
    qv = q.reshape(batch, length, dil * 512)
    kv = k.reshape(batch, length, dil * kw)
    vv = v.reshape(batch, length, dil * kw)
    cur = lambda w: pl.BlockSpec((None, tq, w), lambda b, r, i: (b, i, r))
    prv = pl.BlockSpec((None, BLOCK, kw), lambda b, r, i: (b, jnp.maximum(i * n_sub - 1, 0), r))
    use_sink = sinks is not None
    in_specs = [cur(512), cur(kw), prv, cur(kw), prv]
    args = [qv, kv, kv, vv, vv]
    if use_sink:
        in_specs = [pl.BlockSpec(memory_space=pltpu.SMEM)] + in_specs
        args = [sinks] + args
    out_specs = [cur(512)]
    out_shape = [jax.ShapeDtypeStruct((batch, length, dil * 512), BF16)]
    if want_lse:
        out_specs.append(cur(512))
        out_shape.append(jax.ShapeDtypeStruct((batch, length, dil * 512), F32))
    res = pl.pallas_call(
        functools.partial(_band_kernel, n_sub=n_sub, kw=kw, max_dist=max_dist, use_sink=use_sink,
                          want_lse=want_lse),
        grid=(batch, dil, length // tq),
        in_specs=in_specs, out_specs=out_specs, out_shape=out_shape,
        compiler_params=_params(3), name="band_attention_d%d" % dil,
    )(*args)
    return [r.reshape(batch * length, dil * 512) for r in res]


def _mla_kernel(q_ref, k_ref, v_ref, o_out, m_s, l_s, acc_s, *, tq, tk):
    i = pl.program_id(1)
    n_full = (i * tq) // tk
    row = lax.broadcasted_iota(jnp.int32, (tq, tk), 0) + i * tq
    col = lax.broadcasted_iota(jnp.int32, (tq, tk), 1) + n_full * tk
    causal = col <= row
    lo = lax.broadcasted_iota(jnp.int32, (tq, LANES), 1) < 64
    lo_k = lax.broadcasted_iota(jnp.int32, (tk, LANES), 1) < 64
    m_s[...] = jnp.full(m_s.shape, NEG, F32)
    l_s[...] = jnp.zeros(l_s.shape, F32)
    acc_s[...] = jnp.zeros(acc_s.shape, F32)

    def step(j, masked):
        start = pl.multiple_of(j * tk, tk)
        for p in range(B_HEADS // 2):
            vt = v_ref[pl.ds(start, tk), p * LANES:(p + 1) * LANES]
            zero = jnp.zeros_like(vt)
            v_cat = jnp.concatenate([jnp.where(lo_k, vt, zero), jnp.where(lo_k, zero, vt)], axis=0)
            probs, alphas = [], []
            for hh in range(2):
                h = 2 * p + hh
                hcols = slice(h * LANES, (h + 1) * LANES)
                s = _dot_t(q_ref[:, hcols], k_ref[pl.ds(start, tk), hcols])
                if masked:
                    s = jnp.where(causal, s, NEG)
                chunks = [s[:, c:c + LANES] for c in range(0, tk, LANES)]
                m_old = m_s[h]
                m_new = jnp.maximum(m_old, jnp.max(functools.reduce(jnp.maximum, chunks), axis=-1, keepdims=True))
                alpha = jnp.exp2(m_old - m_new)
                es = [jnp.exp2(ch - m_new) for ch in chunks]
                l_s[h] = alpha * l_s[h] + jnp.sum(functools.reduce(jnp.add, es), axis=-1, keepdims=True)
                m_s[h] = m_new
                probs.extend(e.astype(BF16) for e in es)
                alphas.append(alpha)
            pv = _dot(jnp.concatenate(probs, axis=1), v_cat)
            acc_s[p] = jnp.where(lo, alphas[0], alphas[1]) * acc_s[p] + pv

    def body(j, carry):
        step(j, False)
        return carry

    lax.fori_loop(0, n_full, body, 0)
    step(n_full, True)
    for p in range(B_HEADS // 2):
        inv = jnp.where(lo, 1.0 / l_s[2 * p], 1.0 / l_s[2 * p + 1])
        o_out[:, p * LANES:(p + 1) * LANES] = (acc_s[p] * inv).astype(BF16)


def _mla_attention(q, k, v, *, batch):
    t = q.shape[0]
    seq = t // batch
    tq = min(MLA_TQ, seq)
    tk = min(MLA_TK, seq)
    assert tk % tq == 0 and seq % tk == 0
    q3, k3, v3 = (a.reshape(batch, seq, a.shape[1]) for a in (q, k, v))
    o = pl.pallas_call(
        functools.partial(_mla_kernel, tq=tq, tk=tk),
        grid=(batch, seq // tq),
        in_specs=[pl.BlockSpec((None, tq, 1024), lambda b, i: (b, i, 0)),
                  pl.BlockSpec((None, seq, 1024), lambda b, i: (b, 0, 0)),
                  pl.BlockSpec((None, seq, 512), lambda b, i: (b, 0, 0))],
        out_specs=pl.BlockSpec((None, tq, 512), lambda b, i: (b, i, 0)),
        out_shape=jax.ShapeDtypeStruct((batch, seq, 512), BF16),
        scratch_shapes=[pltpu.VMEM((B_HEADS, tq, LANES), F32), pltpu.VMEM((B_HEADS, tq, LANES), F32),
                        pltpu.VMEM((B_HEADS // 2, tq, LANES), F32)],
        compiler_params=_params(2), name="mla_attention",
    )(q3, k3, v3)
    return o.reshape(t, 512)


def _merge_kernel(x_ref, g_ref, wg_ref, bg_ref, oa_ref, ob_ref, oc0_ref, oc1_ref, oc2_ref,
                  l0_ref, l1_ref, l2_ref, om_ref, wb_ref, wo_ref, x_out, *slabs):
    xf = x_ref[...]
    h = _rms_bf16(xf, g_ref[...])
    dils = [dil for _, dil in C_PATTERNS]
    l0, l1, l2 = (_load_gathered(r, slabs[n], dils[n]) for n, r in enumerate((l0_ref, l1_ref, l2_ref)))
    o0, o1, o2 = (_load_gathered(r, slabs[3 + n], dils[n]) for n, r in enumerate((oc0_ref, oc1_ref, oc2_ref)))
    mx = jnp.maximum(jnp.maximum(l0, l1), l2)
    e0, e1, e2 = jnp.exp(l0 - mx), jnp.exp(l1 - mx), jnp.exp(l2 - mx)
    o_c = (e0 * o0 + e1 * o1 + e2 * o2) * (1.0 / (e0 + e1 + e2))
    branches = (oa_ref[...], ob_ref[...], o_c.astype(BF16), om_ref[...])
    acc = jnp.zeros(xf.shape, F32)
    for n in range(N_BRANCH):
        cols = slice(n * D_MODEL, (n + 1) * D_MODEL)
        gate = jax.nn.sigmoid(_dot(h, wg_ref[:, cols]) + bg_ref[:, cols])
        acc = acc + gate * _dot(branches[n], wb_ref[n])
    x_out[...] = xf + _dot(acc.astype(BF16), wo_ref[...])


def _merge(x, g, wg, bg, oa, ob, oc, lse, om, wb, wo):
    t = x.shape[0]
    grouped = [_rows(dil * 512, ROW_TILE // dil) for _, dil in C_PATTERNS]
    return pl.pallas_call(
        _merge_kernel,
        grid=(t // ROW_TILE,),
        in_specs=[_rows(D_MODEL), _resident(g.shape), _resident(wg.shape), _resident(bg.shape)]
                 + [_rows(512)] * 2 + grouped * 2 + [_rows(512), _resident(wb.shape), _resident(wo.shape)],
        out_specs=_rows(D_MODEL),
        out_shape=jax.ShapeDtypeStruct((t, D_MODEL), F32),
        scratch_shapes=[pltpu.VMEM((4, ROW_TILE, LANES), F32)] * 6,
        compiler_params=_params(1), name="merge",
    )(x, g, wg, bg, oa, ob, *oc, *lse, om, wb, wo)


def _mlp_kernel(x_ref, g_ref, wu_ref, wd_ref, x_out):
    xf = x_ref[...]
    h = _rms_bf16(xf, g_ref[...])
    acc = xf
    for c0 in range(0, D_FF, D_MODEL):
        u = jnp.maximum(_dot(h, wu_ref[:, c0:c0 + D_MODEL]), 0.0)
        acc = acc + _dot((u * u).astype(BF16), wd_ref[c0:c0 + D_MODEL, :])
    x_out[...] = acc


def _mlp(x, g, wu, wd):
    t = x.shape[0]
    return pl.pallas_call(
        _mlp_kernel,
        grid=(t // ROW_TILE,),
        in_specs=[_rows(D_MODEL), _resident(g.shape), _resident(wu.shape), _resident(wd.shape)],
        out_specs=_rows(D_MODEL),
        out_shape=jax.ShapeDtypeStruct((t, D_MODEL), F32),
        compiler_params=_params(1), name="mlp",
    )(x, g, wu, wd)


def _block_diag_means(seg_lens):
    m = np.zeros((MXU_DIM, MXU_DIM), np.float32)
    o = 0
    for n in seg_lens:
        m[o:o + n, o:o + n] = 1.0 / n
        o += n
    return jnp.asarray(m, BF16)


def _rope_tables(positions):
    pos = positions.reshape(-1).astype(F32)

    def tab(dim):
        inv = ROPE_THETA ** (-jnp.arange(0, dim, 2, dtype=F32) / dim)
        ang = pos[:, None] * inv
        return jnp.cos(ang), jnp.sin(ang)

    ch, sh = tab(A_HD)
    cr, sr = tab(B_ROPE)
    t = pos.shape[0]
    z = lambda n: jnp.zeros((t, n), F32)
    o = lambda n: jnp.ones((t, n), F32)
    cat = lambda *a: jnp.concatenate(a, axis=1)
    tabs_a = (cat(ch, ch, ch, ch), cat(-sh, z(32), -sh, z(32)), cat(z(32), sh, z(32), sh))
    tabs_b = (cat(o(64), cr, cr, o(32)), cat(z(64), -sr, z(48)), cat(z(80), sr, z(32)))
    return tabs_a, tabs_b


_A_PERM = np.array([(hh * 4 + p) * A_HD + d for p in range(4) for hh in range(2) for d in range(A_HD)])
_A_SINK_PERM = np.array([hh * 4 + p for p in range(4) for hh in range(2)])


def _split_w_in(w_in):
    sizes = ((A_HEADS * A_HD, A_KV_HEADS * A_HD, A_KV_HEADS * A_HD, B_Q_LORA, B_KV_LORA, B_ROPE)
             + (C_HEADS * C_HD,) * (3 * len(C_PATTERNS)) + (M_HEADS * M_HD, N_BRANCH * D_MODEL))
    offs = np.cumsum((0,) + sizes)
    return [w_in[..., offs[i]:offs[i + 1]] for i in range(len(sizes))]


def _pad_heads(w, widths, slot):
    per = sum(widths)
    heads = w.shape[-1] // per
    wh = w.reshape(w.shape[:-1] + (heads, per))
    pad = [(0, 0)] * (wh.ndim - 1) + [(0, slot - per)]
    return jnp.pad(wh, pad).reshape(w.shape[:-1] + (heads * slot,))


def kernel(x, mem, positions, g_mix, w_in, b_gate, a_qn, a_kn, a_sink, b_qa_norm, b_kva_norm,
           b_w_uq, b_w_ukv, b_qn, b_kn, c_qn, c_kn, m_g_mem, m_w_kv, m_qn, m_kn,
           w_branch, w_out, g_mlp, w_up, w_down):
    batch, seq, d = x.shape
    depth = g_mix.shape[0]
    t = batch * seq
    assert d == D_MODEL and seq % ROW_TILE == 0 and seq % (BLOCK * C_PATTERNS[-1][1]) == 0

    parts = _split_w_in(w_in)
    a_q, a_k, a_v, b_cq, b_ckv, b_kr = parts[:6]
    c_parts = parts[6:15]
    m_q, w_gate = parts[15], parts[16]
    w_a = jnp.concatenate([a_q[..., _A_PERM], a_k, a_v], axis=-1).astype(BF16)
    kr_slot = jnp.pad(b_kr, ((0, 0), (0, 0), (B_NOPE, LANES - B_NOPE - B_ROPE)))
    w_b = jnp.concatenate([b_cq, b_ckv, kr_slot], axis=-1).astype(BF16)
    w_uq = _pad_heads(b_w_uq, (B_NOPE, B_ROPE), LANES).astype(BF16)
    ukv = b_w_ukv.reshape(depth, B_KV_LORA, B_HEADS, B_NOPE + B_V)
    w_uk = _pad_heads(ukv[..., :B_NOPE].reshape(depth, B_KV_LORA, -1), (B_NOPE,), LANES).astype(BF16)
    w_uv = ukv[..., B_NOPE:].reshape(depth, B_KV_LORA, B_HEADS * B_V).astype(BF16)
    w_c = jnp.concatenate(c_parts, axis=-1).astype(BF16)
    w_mq = m_q.astype(BF16)
    w_gate = w_gate.astype(BF16)
    w_mkv = m_w_kv.astype(BF16)
    w_br = jnp.concatenate([w_branch[:, :1][:, :, _A_PERM], w_branch[:, 1:]], axis=1).astype(BF16)
    w_o = w_out.astype(BF16)
    w_u = w_up.astype(BF16)
    w_d = w_down.astype(BF16)

    row = lambda a: a.reshape(1, -1).astype(F32)
    seg64 = _block_diag_means((64,) * 4)
    seg_b = _block_diag_means((64, 32, 32) * 2)
    zeros32 = jnp.zeros((depth, LANES - B_NOPE - B_ROPE), F32)
    b_scale = (B_NOPE + B_ROPE) ** -0.5 * LOG2_E
    gq_b = jnp.tile(jnp.concatenate([b_qn * b_scale, zeros32], axis=-1), (1, B_HEADS))
    gk_b = jnp.tile(jnp.concatenate([b_kn, zeros32], axis=-1), (1, B_HEADS))
    gq_a = jnp.tile(a_qn * (A_HD ** -0.5), (1, A_HEADS))
    gk_a = jnp.tile(a_kn, (1, A_KV_HEADS))
    gq_c = jnp.tile(c_qn * (C_HD ** -0.5), (1, 1, C_HEADS))
    gk_c = jnp.tile(c_kn, (1, 1, C_HEADS))
    gq_m = jnp.tile(m_qn * (M_HD ** -0.5), (1, M_HEADS))
    gk_m = jnp.tile(m_kn, (1, M_HEADS))
    sinks = a_sink[:, _A_SINK_PERM].astype(F32)

    tabs_a, tabs_b = _rope_tables(positions)
    xs = x.reshape(t, D_MODEL)
    mem2d = mem.reshape(batch * N_MEM, D_MODEL)

    for l in range(depth):
        g = row(g_mix[l])
        qa, ka, va = _proj_a(xs, g, w_a[l], seg64, row(gq_a[l]), row(gk_a[l]), tabs_a)
        qb, kb, vb = _proj_b(xs, g, w_b[l], row(b_qa_norm[l]), row(b_kva_norm[l]), w_uq[l], w_uk[l],
                             w_uv[l], seg_b, row(gq_b[l]), row(gk_b[l]), tabs_b)
        qkv_c = _proj_c(xs, g, w_c[l], seg64, gq_c[l].astype(F32), gk_c[l].astype(F32), tabs_a)
        mk, mv = _mem_kv(mem2d, row(m_g_mem[l]), w_mkv[l], row(gk_m[l]))
        o_m = _proj_m(xs, g, w_mq[l], row(gq_m[l]), mk, mv, seq)

        (o_a,) = _band_attention(qa, ka, va, sinks[l], batch=batch, dil=1, max_dist=BLOCK - 1,
                                 want_lse=False)
        o_b = _mla_attention(qb, kb, vb, batch=batch)
        o_c, lse_c = [], []
        for grp, (win, dil) in enumerate(C_PATTERNS):
            o_g, lse_g = _band_attention(qkv_c[3 * grp], qkv_c[3 * grp + 1], qkv_c[3 * grp + 2], None,
                                         batch=batch, dil=dil, max_dist=win // dil, want_lse=True)
            o_c.append(o_g)
            lse_c.append(lse_g)

        xs = _merge(xs, g, w_gate[l], row(b_gate[l]), o_a, o_b, o_c, lse_c, o_m, w_br[l], w_o[l])
        xs = _mlp(xs, row(g_mlp[l]), w_u[l], w_d[l])
    return xs.reshape(batch, seq, D_MODEL)
```

```python
import functools

import numpy as np
import jax
import jax.numpy as jnp
from jax import lax
from jax.experimental import pallas as pl
from jax.experimental.pallas import tpu as pltpu

F32 = jnp.float32
BF16 = jnp.bfloat16

D_MODEL = 1024
N_MEM = 256
BLOCK = 128
ROPE_THETA = 10000.0
EPS = 1e-6
NEG = -1e30
LOG2_E = 1.4426950408889634
LN_2 = 0.6931471805599453

A_HEADS, A_KV_HEADS, A_HD = 8, 2, 64
B_HEADS, B_Q_LORA, B_KV_LORA, B_NOPE, B_ROPE, B_V = 8, 384, 256, 64, 32, 64
C_PATTERNS = ((128, 1), (512, 4), (2048, 16))
C_HEADS, C_HD = 8, 64
M_HEADS, M_HD = 4, 128
N_BRANCH = 4
BRANCH_W = 512
D_FF = 4 * D_MODEL

LANES = 128
MXU_DIM = 256
VMEM_LIMIT = 56 * 1024 * 1024
ROW_TILE = 512
MLA_TQ = 512
MLA_TK = 512


def _dot(a, b):
    return jnp.dot(a, b, preferred_element_type=F32)


def _dot_t(a, b):
    return lax.dot_general(a, b, (((1,), (1,)), ((), ())), preferred_element_type=F32)


def _resident(shape):
    nd = len(shape)
    return pl.BlockSpec(shape, lambda *_: (0,) * nd, pipeline_mode=pl.Buffered(1))


def _rows(width, tm=None):
    return pl.BlockSpec((tm or ROW_TILE, width), lambda i: (i, 0))


def _params(n_axes):
    return pltpu.CompilerParams(dimension_semantics=("arbitrary",) * n_axes,
                                vmem_limit_bytes=VMEM_LIMIT)


def _rms_bf16(xf, g):
    y = xf * lax.rsqrt(jnp.mean(xf * xf, axis=-1, keepdims=True) + EPS)
    return (y * g).astype(BF16)


def _seg_norm(z, seg, gain):
    sq = (z * z).astype(BF16)
    w = z.shape[1]
    if w <= MXU_DIM:
        ms = _dot(sq, seg[:w, :w])
    else:
        ms = jnp.concatenate([_dot(sq[:, c:c + MXU_DIM], seg) for c in range(0, w, MXU_DIM)], axis=1)
    return z * lax.rsqrt(ms + EPS) * gain


def _rope(z, c, s1, s2, half):
    parts = []
    for c0 in range(0, z.shape[1], LANES):
        zc = z[:, c0:c0 + LANES]
        parts.append(zc * c + pltpu.roll(zc, LANES - half, 1) * s1 + pltpu.roll(zc, half, 1) * s2)
    return parts[0] if len(parts) == 1 else jnp.concatenate(parts, axis=1)


def _proj_a_kernel(x_ref, g_ref, w_ref, seg_ref, gq_ref, gk_ref, c_ref, s1_ref, s2_ref,
                   q_out, k_out, v_out):
    h = _rms_bf16(x_ref[...], g_ref[...])
    z = _dot(h, w_ref[...])
    c, s1, s2 = c_ref[...], s1_ref[...], s2_ref[...]
    seg = seg_ref[...]
    q = _rope(_seg_norm(z[:, :512], seg, gq_ref[...]), c, s1, s2, A_HD // 2)
    k = _rope(_seg_norm(z[:, 512:640], seg, gk_ref[...]), c, s1, s2, A_HD // 2)
    q_out[...] = q.astype(BF16)
    k_out[...] = k.astype(BF16)
    v_out[...] = z[:, 640:768].astype(BF16)


def _proj_a(x, g, w, seg, gq, gk, tabs):
    t = x.shape[0]
    return pl.pallas_call(
        _proj_a_kernel,
        grid=(t // ROW_TILE,),
        in_specs=[_rows(D_MODEL), _resident(g.shape), _resident(w.shape), _resident(seg.shape),
                  _resident(gq.shape), _resident(gk.shape), _rows(LANES), _rows(LANES), _rows(LANES)],
        out_specs=[_rows(512), _rows(128), _rows(128)],
        out_shape=[jax.ShapeDtypeStruct((t, 512), BF16), jax.ShapeDtypeStruct((t, 128), BF16),
                   jax.ShapeDtypeStruct((t, 128), BF16)],
        compiler_params=_params(1), name="proj_a",
    )(x, g, w, seg, gq, gk, *tabs)


def _proj_b_kernel(x_ref, g_ref, w_ref, gqa_ref, gkva_ref, wuq_ref, wuk_ref, wuv_ref, seg_ref,
                   gq_ref, gk_ref, c_ref, s1_ref, s2_ref, q_out, k_out, v_out):
    h = _rms_bf16(x_ref[...], g_ref[...])
    z = _dot(h, w_ref[...])
    cq = _rms_bf16(z[:, :B_Q_LORA], gqa_ref[...])
    ckv = _rms_bf16(z[:, B_Q_LORA:B_Q_LORA + B_KV_LORA], gkva_ref[...])
    kr = z[:, B_Q_LORA + B_KV_LORA:]
    q_up = _dot(cq, wuq_ref[...])
    k_up = _dot(ckv, wuk_ref[...]) + jnp.concatenate([kr] * B_HEADS, axis=1)
    v_out[...] = _dot(ckv, wuv_ref[...]).astype(BF16)
    c, s1, s2 = c_ref[...], s1_ref[...], s2_ref[...]
    seg = seg_ref[...]
    q = _rope(_seg_norm(q_up, seg, gq_ref[...]), c, s1, s2, B_ROPE // 2)
    k = _rope(_seg_norm(k_up, seg, gk_ref[...]), c, s1, s2, B_ROPE // 2)
    q_out[...] = q.astype(BF16)
    k_out[...] = k.astype(BF16)


def _proj_b(x, g, w, gqa, gkva, wuq, wuk, wuv, seg, gq, gk, tabs):
    t = x.shape[0]
    consts = (g, w, gqa, gkva, wuq, wuk, wuv, seg, gq, gk)
    return pl.pallas_call(
        _proj_b_kernel,
        grid=(t // ROW_TILE,),
        in_specs=[_rows(D_MODEL)] + [_resident(a.shape) for a in consts] + [_rows(LANES)] * 3,
        out_specs=[_rows(1024), _rows(1024), _rows(512)],
        out_shape=[jax.ShapeDtypeStruct((t, 1024), BF16), jax.ShapeDtypeStruct((t, 1024), BF16),
                   jax.ShapeDtypeStruct((t, 512), BF16)],
        compiler_params=_params(1), name="proj_b",
    )(x, *consts, *tabs)


def _store_gathered(out_ref, val, slab, dil):
    if dil == 1:
        out_ref[...] = val.astype(BF16)
        return
    n = val.shape[0] // dil
    for c in range(4):
        slab[c] = val[:, c * LANES:(c + 1) * LANES]
    for r in range(dil):
        for c in range(4):
            lane0 = r * 512 + c * LANES
            out_ref[:, lane0:lane0 + LANES] = slab[c, pl.ds(r, n, stride=dil), :].astype(BF16)


def _load_gathered(in_ref, slab, dil):
    if dil == 1:
        return in_ref[...].astype(F32)
    n = in_ref.shape[0]
    for r in range(dil):
        for c in range(4):
            lane0 = r * 512 + c * LANES
            slab[c, pl.ds(r, n, stride=dil), :] = in_ref[:, lane0:lane0 + LANES].astype(F32)
    return jnp.concatenate([slab[c] for c in range(4)], axis=1)


def _proj_c_kernel(x_ref, g_ref, w_ref, seg_ref, gq_ref, gk_ref, c_ref, s1_ref, s2_ref, *refs):
    outs, slabs = refs[:9], refs[9:]
    h = _rms_bf16(x_ref[...], g_ref[...])
    c, s1, s2 = c_ref[...], s1_ref[...], s2_ref[...]
    seg = seg_ref[...]
    for grp, (_, dil) in enumerate(C_PATTERNS):
        z = _dot(h, w_ref[:, grp * 1536:(grp + 1) * 1536])
        q = _rope(_seg_norm(z[:, :512], seg, gq_ref[grp:grp + 1, :]), c, s1, s2, C_HD // 2)
        k = _rope(_seg_norm(z[:, 512:1024], seg, gk_ref[grp:grp + 1, :]), c, s1, s2, C_HD // 2)
        _store_gathered(outs[3 * grp], q, slabs[0], dil)
        _store_gathered(outs[3 * grp + 1], k, slabs[1], dil)
        _store_gathered(outs[3 * grp + 2], z[:, 1024:], slabs[2], dil)


def _proj_c(x, g, w, seg, gq, gk, tabs):
    t = x.shape[0]
    out_specs, out_shape = [], []
    for _, dil in C_PATTERNS:
        out_specs += [_rows(dil * 512, ROW_TILE // dil)] * 3
        out_shape += [jax.ShapeDtypeStruct((t // dil, dil * 512), BF16)] * 3
    return pl.pallas_call(
        _proj_c_kernel,
        grid=(t // ROW_TILE,),
        in_specs=[_rows(D_MODEL), _resident(g.shape), _resident(w.shape), _resident(seg.shape),
                  _resident(gq.shape), _resident(gk.shape), _rows(LANES), _rows(LANES), _rows(LANES)],
        out_specs=out_specs, out_shape=out_shape,
        scratch_shapes=[pltpu.VMEM((4, ROW_TILE, LANES), F32)] * 3,
        compiler_params=_params(1), name="proj_c",
    )(x, g, w, seg, gq, gk, *tabs)


def _head_norm_128(z, gain):
    parts = []
    for h0 in range(0, z.shape[1], LANES):
        zh = z[:, h0:h0 + LANES]
        parts.append(zh * lax.rsqrt(jnp.mean(zh * zh, axis=-1, keepdims=True) + EPS))
    return jnp.concatenate(parts, axis=1) * gain


def _mem_kv_kernel(mem_ref, g_ref, w_ref, gk_ref, k_out, v_out):
    h = _rms_bf16(mem_ref[...], g_ref[...])
    z = _dot(h, w_ref[...])
    k_out[...] = _head_norm_128(z[:, :512], gk_ref[...]).astype(BF16)
    v_out[...] = z[:, 512:].astype(BF16)


def _mem_kv(mem2d, g, w, gk):
    rows = mem2d.shape[0]
    return pl.pallas_call(
        _mem_kv_kernel,
        grid=(rows // N_MEM,),
        in_specs=[_rows(D_MODEL, N_MEM), _resident(g.shape), _resident(w.shape), _resident(gk.shape)],
        out_specs=[_rows(512, N_MEM), _rows(512, N_MEM)],
        out_shape=[jax.ShapeDtypeStruct((rows, 512), BF16)] * 2,
        compiler_params=_params(1), name="mem_kv",
    )(mem2d, g, w, gk)


def _proj_m_kernel(x_ref, g_ref, w_ref, gq_ref, mk_ref, mv_ref, o_out):
    h = _rms_bf16(x_ref[...], g_ref[...])
    q = _head_norm_128(_dot(h, w_ref[...]), gq_ref[...]).astype(BF16)
    for hd in range(M_HEADS):
        cols = slice(hd * M_HD, (hd + 1) * M_HD)
        s = _dot_t(q[:, cols], mk_ref[:, cols])
        m = jnp.max(s, axis=-1, keepdims=True)
        e = jnp.exp2(s - m)
        l = jnp.sum(e, axis=-1, keepdims=True)
        o_out[:, cols] = (_dot(e.astype(BF16), mv_ref[:, cols]) * (1.0 / l)).astype(BF16)


def _proj_m(x, g, w, gq, mk, mv, seq):
    t = x.shape[0]
    per_batch = seq // ROW_TILE
    mem_spec = pl.BlockSpec((N_MEM, 512), lambda i: (i // per_batch, 0))
    return pl.pallas_call(
        _proj_m_kernel,
        grid=(t // ROW_TILE,),
        in_specs=[_rows(D_MODEL), _resident(g.shape), _resident(w.shape), _resident(gq.shape),
                  mem_spec, mem_spec],
        out_specs=_rows(512),
        out_shape=jax.ShapeDtypeStruct((t, 512), BF16),
        compiler_params=_params(1), name="proj_m",
    )(x, g, w, gq, mk, mv)


def _band_kernel(*refs, n_sub, kw, max_dist, use_sink, want_lse):
    refs = list(refs)
    sink_ref = refs.pop(0) if use_sink else None
    q_ref, kc_ref, kp_ref, vc_ref, vp_ref, o_out = refs[:6]
    lse_out = refs[6] if want_lse else None
    qi = lax.broadcasted_iota(jnp.int32, (BLOCK, 2 * BLOCK), 0)
    kj = lax.broadcasted_iota(jnp.int32, (BLOCK, 2 * BLOCK), 1)
    dist = qi - kj + BLOCK
    band = (dist >= 0) & (dist <= max_dist)
    prev_ok = jnp.where(pl.program_id(2) > 0, BLOCK, 0)
    first_mask = band & (kj + prev_ok >= BLOCK)
    lo = lax.broadcasted_iota(jnp.int32, (BLOCK, LANES), 1) < 64
    if use_sink:
        assert max_dist < BLOCK
        col0 = lax.broadcasted_iota(jnp.int32, (BLOCK, LANES), 1) == 0
        row0 = lax.broadcasted_iota(jnp.int32, (BLOCK, LANES), 0) == 0
    for j in range(n_sub):
        rows = slice(j * BLOCK, (j + 1) * BLOCK)
        prev = slice((j - 1) * BLOCK, j * BLOCK)
        mask = first_mask if j == 0 else band
        for p in range(4):
            kcol = slice(0, LANES) if kw == LANES else slice(p * LANES, (p + 1) * LANES)
            k_prev = kp_ref[:, kcol] if j == 0 else kc_ref[prev, kcol]
            v_prev = vp_ref[:, kcol] if j == 0 else vc_ref[prev, kcol]
            if use_sink:
                v_prev = jnp.where(row0, jnp.zeros_like(v_prev), v_prev)
            kk = jnp.concatenate([k_prev, kc_ref[rows, kcol]], axis=0)
            vv = jnp.concatenate([v_prev, vc_ref[rows, kcol]], axis=0)
            qp = q_ref[rows, p * LANES:(p + 1) * LANES]
            outs, lses = [], []
            for hh in range(2):
                qm = jnp.where(lo if hh == 0 else jnp.logical_not(lo), qp, jnp.zeros_like(qp))
                s = jnp.where(mask, _dot_t(qm, kk), NEG)
                if use_sink:
                    s = jnp.concatenate([jnp.where(col0, sink_ref[2 * p + hh], s[:, :LANES]), s[:, LANES:]],
                                        axis=1)
                m = jnp.max(s, axis=-1, keepdims=True)
                e = jnp.exp2(s - m)
                l = jnp.sum(e, axis=-1, keepdims=True)
                outs.append(_dot(e.astype(BF16), vv) * (1.0 / l))
                lses.append(m * LN_2 + jnp.log(l))
            o_out[rows, p * LANES:(p + 1) * LANES] = jnp.where(lo, outs[0], outs[1]).astype(BF16)
            if want_lse:
                lse_out[rows, p * LANES:(p + 1) * LANES] = jnp.where(
                    lo, jnp.broadcast_to(lses[0], (BLOCK, LANES)), jnp.broadcast_to(lses[1], (BLOCK, LANES)))


def _band_attention(q, k, v, sinks, *, batch, dil, max_dist, want_lse):
    kw = k.shape[1] // dil
    length = q.shape[0] // batch
    tq = min(ROW_TILE, length)
    n_sub = tq // BLOCK
    qv = q.reshape(batch, length, dil * 512)
    kv = k.reshape(batch, length, dil * kw)
    vv = v.reshape(batch, length, dil * kw)
    cur = lambda w: pl.BlockSpec((None, tq, w), lambda b, r, i: (b, i, r))
    prv = pl.BlockSpec((None, BLOCK, kw), lambda b, r, i: (b, jnp.maximum(i * n_sub - 1, 0), r))
    use_sink = sinks is not None
    in_specs = [cur(512), cur(kw), prv, cur(kw), prv]
    args = [qv, kv, kv, vv, vv]
    if use_sink:
        in_specs = [pl.BlockSpec(memory_space=pltpu.SMEM)] + in_specs
        args = [sinks] + args
    out_specs = [cur(512)]
    out_shape = [jax.ShapeDtypeStruct((batch, length, dil * 512), BF16)]
    if want_lse:
        out_specs.append(cur(512))
        out_shape.append(jax.ShapeDtypeStruct((batch, length, dil * 512), F32))
    res = pl.pallas_call(
        functools.partial(_band_kernel, n_sub=n_sub, kw=kw, max_dist=max_dist, use_sink=use_sink,
                          want_lse=want_lse),
        grid=(batch, dil, length // tq),
        in_specs=in_specs, out_specs=out_specs, out_shape=out_shape,
        compiler_params=_params(3), name="band_attention_d%d" % dil,
    )(*args)
    return [r.reshape(batch * length, dil * 512) for r in res]


def _mla_kernel(q_ref, k_ref, v_ref, o_out, m_s, l_s, acc_s, *, tq, tk):
    i = pl.program_id(1)
    n_full = (i * tq) // tk
    row = lax.broadcasted_iota(jnp.int32, (tq, tk), 0) + i * tq
    col = lax.broadcasted_iota(jnp.int32, (tq, tk), 1) + n_full * tk
    causal = col <= row
    lo = lax.broadcasted_iota(jnp.int32, (tq, LANES), 1) < 64
    lo_k = lax.broadcasted_iota(jnp.int32, (tk, LANES), 1) < 64
    m_s[...] = jnp.full(m_s.shape, NEG, F32)
    l_s[...] = jnp.zeros(l_s.shape, F32)
    acc_s[...] = jnp.zeros(acc_s.shape, F32)

    def step(j, masked):
        start = pl.multiple_of(j * tk, tk)
        for p in range(B_HEADS // 2):
            vt = v_ref[pl.ds(start, tk), p * LANES:(p + 1) * LANES]
            zero = jnp.zeros_like(vt)
            v_cat = jnp.concatenate([jnp.where(lo_k, vt, zero), jnp.where(lo_k, zero, vt)], axis=0)
            probs, alphas = [], []
            for hh in range(2):
                h = 2 * p + hh
                hcols = slice(h * LANES, (h + 1) * LANES)
                s = _dot_t(q_ref[:, hcols], k_ref[pl.ds(start, tk), hcols])
                if masked:
                    s = jnp.where(causal, s, NEG)
                chunks = [s[:, c:c + LANES] for c in range(0, tk, LANES)]
                m_old = m_s[h]
                m_new = jnp.maximum(m_old, jnp.max(functools.reduce(jnp.maximum, chunks), axis=-1, keepdims=True))
                alpha = jnp.exp2(m_old - m_new)
                es = [jnp.exp2(ch - m_new) for ch in chunks]
                l_s[h] = alpha * l_s[h] + jnp.sum(functools.reduce(jnp.add, es), axis=-1, keepdims=True)
                m_s[h] = m_new
                probs.extend(e.astype(BF16) for e in es)
                alphas.append(alpha)
            pv = _dot(jnp.concatenate(probs, axis=1), v_cat)
            acc_s[p] = jnp.where(lo, alphas[0], alphas[1]) * acc_s[p] + pv

    def body(j, carry):
        step(j, False)
        return carry

    lax.fori_loop(0, n_full, body, 0)
    step(n_full, True)
    for p in range(B_HEADS // 2):
        inv = jnp.where(lo, 1.0 / l_s[2 * p], 1.0 / l_s[2 * p + 1])
        o_out[:, p * LANES:(p + 1) * LANES] = (acc_s[p] * inv).astype(BF16)


def _mla_attention(q, k, v, *, batch):
    t = q.shape[0]
    seq = t // batch
    tq = min(MLA_TQ, seq)
    tk = min(MLA_TK, seq)
    assert tk % tq == 0 and seq % tk == 0
    q3, k3, v3 = (a.reshape(batch, seq, a.shape[1]) for a in (q, k, v))
    o = pl.pallas_call(
        functools.partial(_mla_kernel, tq=tq, tk=tk),
        grid=(batch, seq // tq),
        in_specs=[pl.BlockSpec((None, tq, 1024), lambda b, i: (b, i, 0)),
                  pl.BlockSpec((None, seq, 1024), lambda b, i: (b, 0, 0)),
                  pl.BlockSpec((None, seq, 512), lambda b, i: (b, 0, 0))],
        out_specs=pl.BlockSpec((None, tq, 512), lambda b, i: (b, i, 0)),
        out_shape=jax.ShapeDtypeStruct((batch, seq, 512), BF16),
        scratch_shapes=[pltpu.VMEM((B_HEADS, tq, LANES), F32), pltpu.VMEM((B_HEADS, tq, LANES), F32),
                        pltpu.VMEM((B_HEADS // 2, tq, LANES), F32)],
        compiler_params=_params(2), name="mla_attention",
    )(q3, k3, v3)
    return o.reshape(t, 512)


def _merge_kernel(x_ref, g_ref, wg_ref, bg_ref, oa_ref, ob_ref, oc0_ref, oc1_ref, oc2_ref,
                  l0_ref, l1_ref, l2_ref, om_ref, wb_ref, wo_ref, x_out, *slabs):
    xf = x_ref[...]
    h = _rms_bf16(xf, g_ref[...])
    dils = [dil for _, dil in C_PATTERNS]
    l0, l1, l2 = (_load_gathered(r, slabs[n], dils[n]) for n, r in enumerate((l0_ref, l1_ref, l2_ref)))
    o0, o1, o2 = (_load_gathered(r, slabs[3 + n], dils[n]) for n, r in enumerate((oc0_ref, oc1_ref, oc2_ref)))
    mx = jnp.maximum(jnp.maximum(l0, l1), l2)
    e0, e1, e2 = jnp.exp(l0 - mx), jnp.exp(l1 - mx), jnp.exp(l2 - mx)
    o_c = (e0 * o0 + e1 * o1 + e2 * o2) * (1.0 / (e0 + e1 + e2))
    branches = (oa_ref[...], ob_ref[...], o_c.astype(BF16), om_ref[...])
    acc = jnp.zeros(xf.shape, F32)
    for n in range(N_BRANCH):
        cols = slice(n * D_MODEL, (n + 1) * D_MODEL)
        gate = jax.nn.sigmoid(_dot(h, wg_ref[:, cols]) + bg_ref[:, cols])
        acc = acc + gate * _dot(branches[n], wb_ref[n])
    x_out[...] = xf + _dot(acc.astype(BF16), wo_ref[...])


def _merge(x, g, wg, bg, oa, ob, oc, lse, om, wb, wo):
    t = x.shape[0]
    grouped = [_rows(dil * 512, ROW_TILE // dil) for _, dil in C_PATTERNS]
    return pl.pallas_call(
        _merge_kernel,
        grid=(t // ROW_TILE,),
        in_specs=[_rows(D_MODEL), _resident(g.shape), _resident(wg.shape), _resident(bg.shape)]
                 + [_rows(512)] * 2 + grouped * 2 + [_rows(512), _resident(wb.shape), _resident(wo.shape)],
        out_specs=_rows(D_MODEL),
        out_shape=jax.ShapeDtypeStruct((t, D_MODEL), F32),
        scratch_shapes=[pltpu.VMEM((4, ROW_TILE, LANES), F32)] * 6,
        compiler_params=_params(1), name="merge",
    )(x, g, wg, bg, oa, ob, *oc, *lse, om, wb, wo)


def _mlp_kernel(x_ref, g_ref, wu_ref, wd_ref, x_out):
    xf = x_ref[...]
    h = _rms_bf16(xf, g_ref[...])
    acc = xf
    for c0 in range(0, D_FF, D_MODEL):
        u = jnp.maximum(_dot(h, wu_ref[:, c0:c0 + D_MODEL]), 0.0)
        acc = acc + _dot((u * u).astype(BF16), wd_ref[c0:c0 + D_MODEL, :])
    x_out[...] = acc


def _mlp(x, g, wu, wd):
    t = x.shape[0]
    return pl.pallas_call(
        _mlp_kernel,
        grid=(t // ROW_TILE,),
        in_specs=[_rows(D_MODEL), _resident(g.shape), _resident(wu.shape), _resident(wd.shape)],
        out_specs=_rows(D_MODEL),
        out_shape=jax.ShapeDtypeStruct((t, D_MODEL), F32),
        compiler_params=_params(1), name="mlp",
    )(x, g, wu, wd)


def _block_diag_means(seg_lens):
    m = np.zeros((MXU_DIM, MXU_DIM), np.float32)
    o = 0
    for n in seg_lens:
        m[o:o + n, o:o + n] = 1.0 / n
        o += n
    return jnp.asarray(m, BF16)


def _rope_tables(positions):
    pos = positions.reshape(-1).astype(F32)

    def tab(dim):
        inv = ROPE_THETA ** (-jnp.arange(0, dim, 2, dtype=F32) / dim)
        ang = pos[:, None] * inv
        return jnp.cos(ang), jnp.sin(ang)

    ch, sh = tab(A_HD)
    cr, sr = tab(B_ROPE)
    t = pos.shape[0]
    z = lambda n: jnp.zeros((t, n), F32)
    o = lambda n: jnp.ones((t, n), F32)
    cat = lambda *a: jnp.concatenate(a, axis=1)
    tabs_a = (cat(ch, ch, ch, ch), cat(-sh, z(32), -sh, z(32)), cat(z(32), sh, z(32), sh))
    tabs_b = (cat(o(64), cr, cr, o(32)), cat(z(64), -sr, z(48)), cat(z(80), sr, z(32)))
    return tabs_a, tabs_b


_A_PERM = np.array([(hh * 4 + p) * A_HD + d for p in range(4) for hh in range(2) for d in range(A_HD)])
_A_SINK_PERM = np.array([hh * 4 + p for p in range(4) for hh in range(2)])


def _split_w_in(w_in):
    sizes = ((A_HEADS * A_HD, A_KV_HEADS * A_HD, A_KV_HEADS * A_HD, B_Q_LORA, B_KV_LORA, B_ROPE)
             + (C_HEADS * C_HD,) * (3 * len(C_PATTERNS)) + (M_HEADS * M_HD, N_BRANCH * D_MODEL))
    offs = np.cumsum((0,) + sizes)
    return [w_in[..., offs[i]:offs[i + 1]] for i in range(len(sizes))]


def _pad_heads(w, widths, slot):
    per = sum(widths)
    heads = w.shape[-1] // per
    wh = w.reshape(w.shape[:-1] + (heads, per))
    pad = [(0, 0)] * (wh.ndim - 1) + [(0, slot - per)]
    return jnp.pad(wh, pad).reshape(w.shape[:-1] + (heads * slot,))


def kernel(x, mem, positions, g_mix, w_in, b_gate, a_qn, a_kn, a_sink, b_qa_norm, b_kva_norm,
           b_w_uq, b_w_ukv, b_qn, b_kn, c_qn, c_kn, m_g_mem, m_w_kv, m_qn, m_kn,
           w_branch, w_out, g_mlp, w_up, w_down):
    batch, seq, d = x.shape
    depth = g_mix.shape[0]
    t = batch * seq
    assert d == D_MODEL and seq % ROW_TILE == 0 and seq % (BLOCK * C_PATTERNS[-1][1]) == 0

    parts = _split_w_in(w_in)
    a_q, a_k, a_v, b_cq, b_ckv, b_kr = parts[:6]
    c_parts = parts[6:15]
    m_q, w_gate = parts[15], parts[16]
    w_a = jnp.concatenate([a_q[..., _A_PERM], a_k, a_v], axis=-1).astype(BF16)
    kr_slot = jnp.pad(b_kr, ((0, 0), (0, 0), (B_NOPE, LANES - B_NOPE - B_ROPE)))
    w_b = jnp.concatenate([b_cq, b_ckv, kr_slot], axis=-1).astype(BF16)
    w_uq = _pad_heads(b_w_uq, (B_NOPE, B_ROPE), LANES).astype(BF16)
    ukv = b_w_ukv.reshape(depth, B_KV_LORA, B_HEADS, B_NOPE + B_V)
    w_uk = _pad_heads(ukv[..., :B_NOPE].reshape(depth, B_KV_LORA, -1), (B_NOPE,), LANES).astype(BF16)
    w_uv = ukv[..., B_NOPE:].reshape(depth, B_KV_LORA, B_HEADS * B_V).astype(BF16)
    w_c = jnp.concatenate(c_parts, axis=-1).astype(BF16)
    w_mq = m_q.astype(BF16)
    w_gate = w_gate.astype(BF16)
    w_mkv = m_w_kv.astype(BF16)
    w_br = jnp.concatenate([w_branch[:, :1][:, :, _A_PERM], w_branch[:, 1:]], axis=1).astype(BF16)
    w_o = w_out.astype(BF16)
    w_u = w_up.astype(BF16)
    w_d = w_down.astype(BF16)

    row = lambda a: a.reshape(1, -1).astype(F32)
    seg64 = _block_diag_means((64,) * 4)
    seg_b = _block_diag_means((64, 32, 32) * 2)
    zeros32 = jnp.zeros((depth, LANES - B_NOPE - B_ROPE), F32)
    b_scale = (B_NOPE + B_ROPE) ** -0.5 * LOG2_E
    gq_b = jnp.tile(jnp.concatenate([b_qn * b_scale, zeros32], axis=-1), (1, B_HEADS))
    gk_b = jnp.tile(jnp.concatenate([b_kn, zeros32], axis=-1), (1, B_HEADS))
    gq_a = jnp.tile(a_qn * (A_HD ** -0.5 * LOG2_E), (1, A_HEADS))
    gk_a = jnp.tile(a_kn, (1, A_KV_HEADS))
    gq_c = jnp.tile(c_qn * (C_HD ** -0.5 * LOG2_E), (1, 1, C_HEADS))
    gk_c = jnp.tile(c_kn, (1, 1, C_HEADS))
    gq_m = jnp.tile(m_qn * (M_HD ** -0.5 * LOG2_E), (1, M_HEADS))
    gk_m = jnp.tile(m_kn, (1, M_HEADS))
    sinks = a_sink[:, _A_SINK_PERM].astype(F32) * LOG2_E

    tabs_a, tabs_b = _rope_tables(positions)
    xs = x.reshape(t, D_MODEL)
    mem2d = mem.reshape(batch * N_MEM, D_MODEL)

    for l in range(depth):
        g = row(g_mix[l])
        qa, ka, va = _proj_a(xs, g, w_a[l], seg64, row(gq_a[l]), row(gk_a[l]), tabs_a)
        qb, kb, vb = _proj_b(xs, g, w_b[l], row(b_qa_norm[l]), row(b_kva_norm[l]), w_uq[l], w_uk[l],
                             w_uv[l], seg_b, row(gq_b[l]), row(gk_b[l]), tabs_b)
        qkv_c = _proj_c(xs, g, w_c[l], seg64, gq_c[l].astype(F32), gk_c[l].astype(F32), tabs_a)
        mk, mv = _mem_kv(mem2d, row(m_g_mem[l]), w_mkv[l], row(gk_m[l]))
        o_m = _proj_m(xs, g, w_mq[l], row(gq_m[l]), mk, mv, seq)

        (o_a,) = _band_attention(qa, ka, va, sinks[l], batch=batch, dil=1, max_dist=BLOCK - 1,
                                 want_lse=False)
        o_b = _mla_attention(qb, kb, vb, batch=batch)
        o_c, lse_c = [], []
        for grp, (win, dil) in enumerate(C_PATTERNS):
            o_g, lse_g = _band_attention(qkv_c[3 * grp], qkv_c[3 * grp + 1], qkv_c[3 * grp + 2], None,
                                         batch=batch, dil=dil, max_dist=win // dil, want_lse=True)
            o_c.append(o_g)
            lse_c.append(lse_g)

        xs = _merge(xs, g, w_gate[l], row(b_gate[l]), o_a, o_b, o_c, lse_c, o_m, w_br[l], w_o[l])
        xs = _mlp(xs, row(g_mlp[l]), w_u[l], w_d[l])
    return xs.reshape(batch, seq, D_MODEL)
```
